```python
import math
import jax, jax.numpy as jnp
from jax import lax
import numpy as np

D_MODEL = 2048
BATCH = 2
SEQ = 4096
DEPTH = 4

GRID_W = 64
CTX_LEN = 256
N_MIXERS = 4
EPS = 1e-6
MOD_CHUNKS = 6
D_FF = 5632
FFN_CONV_WIDTH = 3
GQA_HEADS = 16
GQA_KV_HEADS = 4
HEAD_DIM = 128
ROPE_THETA = 10000.0
Q_BLOCK = 128
MLSTM_HEADS = 8
MLSTM_DQK = 128
MLSTM_DV = 256
MLSTM_CHUNK = 64
M_INIT = -1e30
CONV_INNER = D_MODEL
CONV_WIDTH = 31
NA_HEADS = 16
NA_HEAD_DIM = 128
NA_ROWS = 8
NA_COLS = 16

kernel_name = "hybrid_interleaved_diffusion_block"

F32 = jnp.float32


def rms_norm(x, w):
    x32 = x.astype(F32)
    y = x32 * lax.rsqrt(jnp.mean(x32 * x32, axis=-1, keepdims=True) + EPS)
    return (y * w.astype(F32)).astype(x.dtype)


def layer_norm(x, w, b):
    x32 = x.astype(F32)
    mu = jnp.mean(x32, axis=-1, keepdims=True)
    xc = x32 - mu
    var = jnp.mean(xc * xc, axis=-1, keepdims=True)
    return (xc * lax.rsqrt(var + EPS) * w.astype(F32) + b.astype(F32)).astype(x.dtype)


def modulate(x, shift, scale):
    return x * (1 + scale) + shift


def depthwise_conv(x, w, b):
    width = w.shape[0]
    y = lax.conv_general_dilated(
        x, w[:, None, :].astype(x.dtype), window_strides=(1,),
        padding=[(width // 2, width // 2)],
        dimension_numbers=("NWC", "WIO", "NWC"),
        feature_group_count=x.shape[-1])
    return y + b


def rope_1d(x, pos):
    half = x.shape[-1] // 2
    freqs = ROPE_THETA ** (-jnp.arange(half, dtype=F32) / half)
    ang = pos.astype(F32)[:, None] * freqs[None, :]
    cos = jnp.cos(ang)[:, None, :]
    sin = jnp.sin(ang)[:, None, :]
    x32 = x.astype(F32)
    x1, x2 = x32[..., :half], x32[..., half:]
    return jnp.concatenate([x1 * cos - x2 * sin, x1 * sin + x2 * cos], axis=-1).astype(x.dtype)


def rope_axial(x, row, col):
    h = x.shape[-1] // 2
    return jnp.concatenate([rope_1d(x[..., :h], row), rope_1d(x[..., h:], col)], axis=-1)


def grouped_attend(q, k, v):
    s = jnp.einsum("bqkgd,bskd->bkgqs", q, k).astype(F32)
    p = jax.nn.softmax(s, axis=-1).astype(v.dtype)
    return jnp.einsum("bkgqs,bskd->bqkgd", p, v)


def gqa_axial_mixer(h_lat, h_ctx, w_in, q_norm, k_norm, w_out, need_ctx):
    B, S, _ = h_lat.shape
    n_ctx = h_ctx.shape[1]
    G = GQA_HEADS // GQA_KV_HEADS
    nq = GQA_HEADS * HEAD_DIM
    pos = jnp.arange(S)
    row, col = pos // GRID_W, pos % GRID_W

    def project(h, with_q):
        n = h.shape[1]
        z = h @ (w_in if with_q else w_in[:, nq:])
        q = None
        if with_q:
            q, z = z[..., :nq], z[..., nq:]
            q = rms_norm(q.reshape(B, n, GQA_HEADS, HEAD_DIM), q_norm) * HEAD_DIM ** -0.5
        k, v = jnp.split(z, 2, axis=-1)
        k = rms_norm(k.reshape(B, n, GQA_KV_HEADS, HEAD_DIM), k_norm)
        v = v.reshape(B, n, GQA_KV_HEADS, HEAD_DIM)
        return q, k, v

    q_l, k_l, v_l = project(h_lat, True)
    q_c, k_c, v_c = project(h_ctx, need_ctx)
    q_l = rope_axial(q_l, row, col)
    k_l = rope_axial(k_l, row, col)
    k_all = jnp.concatenate([k_c, k_l], axis=1)
    v_all = jnp.concatenate([v_c, v_l], axis=1)
    nb = S // Q_BLOCK
    q_blocks = q_l.reshape(B, nb, Q_BLOCK, GQA_KV_HEADS, G, HEAD_DIM).transpose(1, 0, 2, 3, 4, 5)
    o = lax.map(lambda qb: grouped_attend(qb, k_all, v_all), q_blocks)
    o = o.transpose(1, 0, 2, 3, 4, 5).reshape(B, S, nq)
    y_lat = o @ w_out
    y_ctx = None
    if need_ctx:
        qc = q_c.reshape(B, n_ctx, GQA_KV_HEADS, G, HEAD_DIM)
        y_ctx = grouped_attend(qc, k_c, v_c).reshape(B, n_ctx, nq) @ w_out
    return y_lat, y_ctx


def mlstm_chunkwise(q, k, v, ig, lf, state, return_h):
    B, T, H, dk = q.shape
    dv = v.shape[-1]
    L = min(MLSTM_CHUNK, T)
    nc = T // L
    out_dtype = v.dtype
    q = (q.astype(F32) * dk ** -0.5).reshape(B, nc, L, H, dk)
    k = k.astype(F32).reshape(B, nc, L, H, dk)
    v = v.astype(F32).reshape(B, nc, L, H, dv)
    ig = ig.reshape(B, nc, L, H)
    lf = lf.reshape(B, nc, L, H)
    b = jnp.cumsum(lf, axis=2)
    g = b[:, :, -1]
    a = g[:, :, None] - b + ig
    m_loc = jnp.max(a, axis=2)
    w = jnp.exp(a - m_loc[:, :, None])
    kv_sum = jnp.einsum("bclh,bclhd,bclhe->bchde", w, k, v)
    k_sum = jnp.einsum("bclh,bclhd->bchd", w, k)

    def step(carry, xs):
        C, n, m = carry
        g_c, ml_c, kv_c, ks_c = xs
        m_new = jnp.maximum(g_c + m, ml_c)
        a_old = jnp.exp(g_c + m - m_new)
        a_new = jnp.exp(ml_c - m_new)
        C_new = a_old[..., None, None] * C + a_new[..., None, None] * kv_c
        n_new = a_old[..., None] * n + a_new[..., None] * ks_c
        return (C_new, n_new, m_new), ((C, n, m) if return_h else None)

    xs = (jnp.moveaxis(g, 1, 0), jnp.moveaxis(m_loc, 1, 0),
          jnp.moveaxis(kv_sum, 1, 0), jnp.moveaxis(k_sum, 1, 0))
    final, prev = lax.scan(step, state, xs)
    if not return_h:
        return None, final
    C_in, n_in, m_in = [jnp.moveaxis(t, 0, 1) for t in prev]
    lower = jnp.tril(jnp.ones((L, L), dtype=bool))
    log_d = b[:, :, :, None, :] - b[:, :, None, :, :] + ig[:, :, None, :, :]
    log_d = jnp.where(lower[None, None, :, :, None], log_d, -jnp.inf)
    log_inter = b + m_in[:, :, None, :]
    m_j = jnp.maximum(log_inter, jnp.max(log_d, axis=3))
    w_intra = jnp.exp(log_d - m_j[:, :, :, None, :])
    w_inter = jnp.exp(log_inter - m_j)
    s = jnp.einsum("bcjhd,bcshd->bcjsh", q, k) * w_intra
    num = (jnp.einsum("bcjsh,bcshe->bcjhe", s, v)
           + w_inter[..., None] * jnp.einsum("bcjhd,bchde->bcjhe", q, C_in))
    den = jnp.sum(s, axis=3) + w_inter * jnp.einsum("bcjhd,bchd->bcjh", q, n_in)
    h = num / jnp.maximum(jnp.abs(den), jnp.exp(-m_j))[..., None]
    return h.reshape(B, T, H, dv).astype(out_dtype), final


def mlstm_mixer(h_lat, h_ctx, w_in, b_gate, out_norm, w_out, need_ctx):
    B = h_lat.shape[0]
    nqk = MLSTM_HEADS * MLSTM_DQK
    nv = MLSTM_HEADS * MLSTM_DV

    def project(h):
        n = h.shape[1]
        q, k, v, o, gates = jnp.split(h @ w_in, [nqk, 2 * nqk, 2 * nqk + nv, 2 * nqk + 2 * nv], axis=-1)
        q = q.reshape(B, n, MLSTM_HEADS, MLSTM_DQK)
        k = k.reshape(B, n, MLSTM_HEADS, MLSTM_DQK)
        v = v.reshape(B, n, MLSTM_HEADS, MLSTM_DV)
        gates = (gates + b_gate).astype(F32).reshape(B, n, 4, MLSTM_HEADS)
        return q, k, v, o, gates

    def finish(h, o):
        n = h.shape[1]
        h = rms_norm(h, out_norm.reshape(MLSTM_HEADS, MLSTM_DV)).reshape(B, n, nv)
        return (h * jax.nn.sigmoid(o)) @ w_out

    def flip(t):
        return jnp.flip(t, axis=1)

    ql, kl, vl, ol, gl = project(h_lat)
    qc, kc, vc, oc, gc = project(h_ctx)
    zero = (jnp.zeros((B, MLSTM_HEADS, MLSTM_DQK, MLSTM_DV), F32),
            jnp.zeros((B, MLSTM_HEADS, MLSTM_DQK), F32),
            jnp.full((B, MLSTM_HEADS), M_INIT, F32))
    lsig = jax.nn.log_sigmoid
    hc_f, st_f = mlstm_chunkwise(qc, kc, vc, gc[:, :, 0], lsig(gc[:, :, 1]), zero, need_ctx)
    hl_f, _ = mlstm_chunkwise(ql, kl, vl, gl[:, :, 0], lsig(gl[:, :, 1]), st_f, True)
    hc_b, st_b = mlstm_chunkwise(flip(qc), flip(kc), flip(vc), flip(gc[:, :, 2]),
                                 lsig(flip(gc[:, :, 3])), zero, need_ctx)
    hl_b, _ = mlstm_chunkwise(flip(ql), flip(kl), flip(vl), flip(gl[:, :, 2]),
                              lsig(flip(gl[:, :, 3])), st_b, True)
    y_lat = finish(hl_f + flip(hl_b), ol)
    y_ctx = finish(hc_f + flip(hc_b), oc) if need_ctx else None
    return y_lat, y_ctx


def conformer_conv_mixer(h_lat, h_ctx, w_pw1, b_pw1, w_dw, b_dw, ln_w, ln_b, w_pw2, b_pw2, need_ctx):
    def branch(h):
        a, gt = jnp.split(h @ w_pw1 + b_pw1, 2, axis=-1)
        u = a * jax.nn.sigmoid(gt)
        u = depthwise_conv(u, w_dw, b_dw)
        u = jax.nn.silu(layer_norm(u, ln_w, ln_b))
        return u @ w_pw2 + b_pw2
    return branch(h_lat), (branch(h_ctx) if need_ctx else None)


def neighbourhood_mixer(h_lat, h_ctx, w_in, rpb, w_out, need_ctx):
    B, S, _ = h_lat.shape
    n_ctx = h_ctx.shape[1]
    rows = S // GRID_W
    wr = min(NA_ROWS, rows)
    nq = NA_HEADS * NA_HEAD_DIM
    scale = NA_HEAD_DIM ** -0.5
    q, k, v = jnp.split(h_lat @ w_in, [nq, 2 * nq], axis=-1)
    q = (q * scale).reshape(B, rows, GRID_W, NA_HEADS, NA_HEAD_DIM).transpose(1, 0, 2, 3, 4)
    k = k.reshape(B, rows, GRID_W, NA_HEADS, NA_HEAD_DIM)
    v = v.reshape(B, rows, GRID_W, NA_HEADS, NA_HEAD_DIM)
    k_c, v_c = jnp.split(h_ctx @ w_in[:, nq:], 2, axis=-1)
    k_c = k_c.reshape(B, n_ctx, NA_HEADS, NA_HEAD_DIM)
    v_c = v_c.reshape(B, n_ctx, NA_HEADS, NA_HEAD_DIM)
    cols = jnp.arange(GRID_W)
    c_start = jnp.clip(cols - NA_COLS // 2, 0, GRID_W - NA_COLS)
    col_idx = c_start[:, None] + jnp.arange(NA_COLS)[None, :]
    col_bias_idx = col_idx - cols[:, None] + (NA_COLS - 1)

    def row_block(args):
        r, q_r = args
        r_start = jnp.clip(r - wr // 2, 0, rows - wr)
        k_band = lax.dynamic_slice_in_dim(k, r_start, wr, axis=1)
        v_band = lax.dynamic_slice_in_dim(v, r_start, wr, axis=1)
        k_nb = k_band[:, :, col_idx]
        v_nb = v_band[:, :, col_idx]
        row_bias_idx = r_start + jnp.arange(wr) - r + (NA_ROWS - 1)
        bias = rpb[:, row_bias_idx[:, None, None], col_bias_idx[None, :, :]]
        s_nb = (jnp.einsum("bqhd,brqjhd->bhqrj", q_r, k_nb).astype(F32)
                + jnp.transpose(bias, (0, 2, 1, 3))[None].astype(F32))
        s_ctx = jnp.einsum("bqhd,bchd->bhqc", q_r, k_c).astype(F32)
        s = jnp.concatenate([s_nb.reshape(B, NA_HEADS, GRID_W, wr * NA_COLS), s_ctx], axis=-1)
        p = jax.nn.softmax(s, axis=-1).astype(v.dtype)
        p_nb = p[..., :wr * NA_COLS].reshape(B, NA_HEADS, GRID_W, wr, NA_COLS)
        p_ctx = p[..., wr * NA_COLS:]
        return (jnp.einsum("bhqrj,brqjhd->bqhd", p_nb, v_nb)
                + jnp.einsum("bhqc,bchd->bqhd", p_ctx, v_c))

    o = lax.map(row_block, (jnp.arange(rows), q))
    y_lat = o.transpose(1, 0, 2, 3, 4).reshape(B, S, nq) @ w_out
    y_ctx = None
    if need_ctx:
        q_c = (h_ctx @ w_in[:, :nq]) * scale
        q_c = q_c.reshape(B, n_ctx, NA_HEADS, 1, NA_HEAD_DIM)
        o_c = jnp.stack([grouped_attend(q_c[:, :, hh:hh + 1], k_c[:, :, hh:hh + 1], v_c[:, :, hh:hh + 1])
                         for hh in range(0)], 0) if False else None
        s_c = jnp.einsum("bqhd,bchd->bhqc", q_c[:, :, :, 0], k_c).astype(F32)
        p_c = jax.nn.softmax(s_c, axis=-1).astype(v_c.dtype)
        y_ctx = jnp.einsum("bhqc,bchd->bqhd", p_c, v_c).reshape(B, n_ctx, nq) @ w_out
    return y_lat, y_ctx


def conv_ffn(h, w_up, conv_w, conv_b, w_down):
    u = depthwise_conv(h @ w_up, conv_w, conv_b)
    val, gate = jnp.split(u, 2, axis=-1)
    return (val * jax.nn.silu(gate)) @ w_down


def setup_inputs(seed: int = 0) -> dict:
    key = jax.random.key(seed)
    keys = jax.random.split(key, 48)
    counter = [0]

    def nrm(shape, scale):
        kk = keys[counter[0]]
        counter[0] += 1
        return jax.random.normal(kk, shape, F32) * scale

    def gain(shape):
        return 1.0 + nrm(shape, 0.05)

    D = D_MODEL
    nA, nB, nC, nD = [len(range(m, DEPTH, N_MIXERS)) for m in range(N_MIXERS)]
    gqa_cols = (GQA_HEADS + 2 * GQA_KV_HEADS) * HEAD_DIM
    ml_cols = 2 * MLSTM_HEADS * MLSTM_DQK + 2 * MLSTM_HEADS * MLSTM_DV + 4 * MLSTM_HEADS
    f_bias = jnp.linspace(3.0, 6.0, MLSTM_HEADS, dtype=F32)
    gate_base = jnp.array([0.0, 1.0, 0.0, 1.0], F32)[:, None] * f_bias[None, :]
    inp = {}
    inp["x"] = nrm((BATCH, SEQ, D), 1.0)
    inp["c"] = nrm((BATCH, D), 1.0)
    inp["ctx"] = nrm((BATCH, CTX_LEN, D), 1.0)
    inp["c_ctx"] = nrm((D,), 1.0)
    inp["mod_w"] = nrm((DEPTH, D, MOD_CHUNKS * D), 0.5 * D ** -0.5)
    inp["mod_b"] = nrm((DEPTH, MOD_CHUNKS * D), 0.02)
    inp["norm_pre_mix"] = gain((DEPTH, D))
    inp["norm_post_mix"] = gain((DEPTH, D))
    inp["norm_pre_ffn"] = gain((DEPTH, D))
    inp["norm_post_ffn"] = gain((DEPTH, D))
    inp["ffn_w_up"] = nrm((DEPTH, D, 2 * D_FF), D ** -0.5)
    inp["ffn_conv_w"] = nrm((DEPTH, FFN_CONV_WIDTH, 2 * D_FF), FFN_CONV_WIDTH ** -0.5)
    inp["ffn_conv_b"] = nrm((DEPTH, 2 * D_FF), 0.02)
    inp["ffn_w_down"] = nrm((DEPTH, D_FF, D), D_FF ** -0.5)
    inp["gqa_w_in"] = nrm((nA, D, gqa_cols), D ** -0.5)
    inp["gqa_q_norm"] = gain((nA, HEAD_DIM))
    inp["gqa_k_norm"] = gain((nA, HEAD_DIM))
    inp["gqa_w_out"] = nrm((nA, GQA_HEADS * HEAD_DIM, D), (GQA_HEADS * HEAD_DIM) ** -0.5)
    inp["mlstm_w_in"] = nrm((nB, D, ml_cols), D ** -0.5)
    inp["mlstm_b_gate"] = (nrm((nB, 4, MLSTM_HEADS), 0.1) + gate_base[None]).reshape(nB, 4 * MLSTM_HEADS)
    inp["mlstm_out_norm"] = gain((nB, MLSTM_HEADS * MLSTM_DV))
    inp["mlstm_w_out"] = nrm((nB, MLSTM_HEADS * MLSTM_DV, D), (MLSTM_HEADS * MLSTM_DV) ** -0.5)
    inp["conv_w_pw1"] = nrm((nC, D, 2 * CONV_INNER), D ** -0.5)
    inp["conv_b_pw1"] = nrm((nC, 2 * CONV_INNER), 0.02)
    inp["conv_w_dw"] = nrm((nC, CONV_WIDTH, CONV_INNER), CONV_WIDTH ** -0.5)
    inp["conv_b_dw"] = nrm((nC, CONV_INNER), 0.02)
    inp["conv_ln_w"] = gain((nC, CONV_INNER))
    inp["conv_ln_b"] = nrm((nC, CONV_INNER), 0.02)
    inp["conv_w_pw2"] = nrm((nC, CONV_INNER, D), CONV_INNER ** -0.5)
    inp["conv_b_pw2"] = nrm((nC, D), 0.02)
    inp["nat_w_in"] = nrm((nD, D, 3 * NA_HEADS * NA_HEAD_DIM), D ** -0.5)
    inp["nat_rpb"] = nrm((nD, NA_HEADS, 2 * NA_ROWS - 1, 2 * NA_COLS - 1), 0.1)
    inp["nat_w_out"] = nrm((nD, NA_HEADS * NA_HEAD_DIM, D), (NA_HEADS * NA_HEAD_DIM) ** -0.5)
    return inp


def reference(x, c, ctx, c_ctx, mod_w, mod_b, norm_pre_mix, norm_post_mix, norm_pre_ffn, norm_post_ffn,
              ffn_w_up, ffn_conv_w, ffn_conv_b, ffn_w_down,
              gqa_w_in, gqa_q_norm, gqa_k_norm, gqa_w_out,
              mlstm_w_in, mlstm_b_gate, mlstm_out_norm, mlstm_w_out,
              conv_w_pw1, conv_b_pw1, conv_w_dw, conv_b_dw, conv_ln_w, conv_ln_b, conv_w_pw2, conv_b_pw2,
              nat_w_in, nat_rpb, nat_w_out):
    xc = ctx
    for i in range(DEPTH):
        kind = i % N_MIXERS
        j = i // N_MIXERS
        need_ctx = i < DEPTH - 1
        mod = jax.nn.silu(c) @ mod_w[i] + mod_b[i]
        mod_c = jax.nn.silu(c_ctx) @ mod_w[i] + mod_b[i]
        sh1, sc1, g1, sh2, sc2, g2 = jnp.split(mod[:, None, :], MOD_CHUNKS, axis=-1)
        csh1, csc1, cg1, csh2, csc2, cg2 = jnp.split(mod_c, MOD_CHUNKS)
        h_l = modulate(rms_norm(x, norm_pre_mix[i]), sh1, sc1)
        h_c = modulate(rms_norm(xc, norm_pre_mix[i]), csh1, csc1)
        if kind == 0:
            y_l, y_c = gqa_axial_mixer(h_l, h_c, gqa_w_in[j], gqa_q_norm[j], gqa_k_norm[j], gqa_w_out[j], need_ctx)
        elif kind == 1:
            y_l, y_c = mlstm_mixer(h_l, h_c, mlstm_w_in[j], mlstm_b_gate[j], mlstm_out_norm[j], mlstm_w_out[j], need_ctx)
        elif kind == 2:
            y_l, y_c = conformer_conv_mixer(h_l, h_c, conv_w_pw1[j], conv_b_pw1[j], conv_w_dw[j], conv_b_dw[j],
                                            conv_ln_w[j], conv_ln_b[j], conv_w_pw2[j], conv_b_pw2[j], need_ctx)
        else:
            y_l, y_c = neighbourhood_mixer(h_l, h_c, nat_w_in[j], nat_rpb[j], nat_w_out[j], need_ctx)
        x = x + g1 * rms_norm(y_l, norm_post_mix[i])
        h_l = modulate(rms_norm(x, norm_pre_ffn[i]), sh2, sc2)
        x = x + g2 * rms_norm(conv_ffn(h_l, ffn_w_up[i], ffn_conv_w[i], ffn_conv_b[i], ffn_w_down[i]), norm_post_ffn[i])
        if need_ctx:
            xc = xc + cg1 * rms_norm(y_c, norm_post_mix[i])
            h_c = modulate(rms_norm(xc, norm_pre_ffn[i]), csh2, csc2)
            xc = xc + cg2 * rms_norm(conv_ffn(h_c, ffn_w_up[i], ffn_conv_w[i], ffn_conv_b[i], ffn_w_down[i]),
                                     norm_post_ffn[i])
    return x
```

```python
import functools

import numpy as np
import jax
import jax.numpy as jnp
from jax import lax
from jax.experimental import pallas as pl
from jax.experimental.pallas import tpu as pltpu

F32 = jnp.float32
BF16 = jnp.bfloat16

D_MODEL = 2048
BATCH = 2
SEQ = 4096
DEPTH = 4
GRID_W = 64
CTX_LEN = 256
EPS = 1e-6
MOD_CHUNKS = 6
D_FF = 5632
GQA_HEADS = 16
GQA_KV_HEADS = 4
HEAD_DIM = 128
ROPE_THETA = 10000.0
MLSTM_HEADS = 8
MLSTM_DQK = 128
MLSTM_DV = 256
M_INIT = -1e30
CONV_WIDTH = 31
NA_HEADS = 16
NA_HEAD_DIM = 128
NA_ROWS = 8
NA_COLS = 16

R_LAT = BATCH * SEQ
R_ALL = R_LAT + BATCH * CTX_LEN
NEG = -1e30

VMEM_LIMIT = 56 * 1024 * 1024
SEG = 256
HALO = 16


def _cparams(sem):
    return pltpu.CompilerParams(dimension_semantics=sem, vmem_limit_bytes=VMEM_LIMIT)


def _sigmoid(x):
    return 1.0 / (1.0 + jnp.exp(-x))


def _rms(x, w):
    return x * lax.rsqrt(jnp.mean(x * x, axis=-1, keepdims=True) + EPS) * w


def _dot(a, b):
    return jnp.dot(a, b, preferred_element_type=F32)


def _dot_nt(a, b):
    return lax.dot_general(a, b, (((1,), (1,)), ((), ())), preferred_element_type=F32)


def _dot_tn(a, b):
    return lax.dot_general(a, b, (((0,), (0,)), ((), ())), preferred_element_type=F32)


def _is_seq_start(r):
    return (r & jnp.where(r < R_LAT, SEQ - 1, CTX_LEN - 1)) == 0


def _mod_index(tm):
    per_batch = SEQ // tm
    return lambda i: jnp.minimum(i // per_batch, BATCH)


def _mod_kernel(a_ref, w_ref, b_ref, o_ref):
    a = a_ref[...]
    s = (a * _sigmoid(a)).astype(BF16)
    o_ref[...] = _dot(s, w_ref[...].astype(BF16)) + b_ref[...]


def _modulation(a8, mod_w, mod_b):
    tn = 1024
    n = MOD_CHUNKS * D_MODEL
    return pl.pallas_call(
        _mod_kernel,
        grid=(DEPTH, n // tn),
        in_specs=[
            pl.BlockSpec((8, D_MODEL), lambda l, j: (0, 0)),
            pl.BlockSpec((None, D_MODEL, tn), lambda l, j: (l, 0, j)),
            pl.BlockSpec((None, 1, tn), lambda l, j: (l, 0, j)),
        ],
        out_specs=pl.BlockSpec((None, 8, tn), lambda l, j: (l, 0, j)),
        out_shape=jax.ShapeDtypeStruct((DEPTH, 8, n), F32),
        compiler_params=_cparams(("parallel", "parallel")),
        name="modulation",
    )(a8, mod_w, mod_b.reshape(DEPTH, 1, n))


def _norm_mod_to_scratch(x_ref, nw_ref, sh_ref, sc_ref, h_scr):
    @pl.when(pl.program_id(1) == 0)
    def _():
        h = _rms(x_ref[...], nw_ref[...]) * (1.0 + sc_ref[...]) + sh_ref[...]
        h_scr[...] = h.astype(BF16)


def _proj_plain_kernel(x_ref, nw_ref, sh_ref, sc_ref, w_ref, b_ref, o_ref, h_scr, *, n_scaled, scale):
    _norm_mod_to_scratch(x_ref, nw_ref, sh_ref, sc_ref, h_scr)
    y = _dot(h_scr[...], w_ref[...]) + b_ref[...]
    if n_scaled:
        y = y * jnp.where(pl.program_id(1) < n_scaled, scale, 1.0).astype(F32)
    o_ref[...] = y.astype(o_ref.dtype)


def _proj_glu_kernel(x_ref, nw_ref, sh_ref, sc_ref, wa_ref, wg_ref, ba_ref, bg_ref, o_ref, h_scr):
    _norm_mod_to_scratch(x_ref, nw_ref, sh_ref, sc_ref, h_scr)
    h = h_scr[...]
    a = _dot(h, wa_ref[...]) + ba_ref[...]
    g = _dot(h, wg_ref[...]) + bg_ref[...]
    o_ref[...] = (a * _sigmoid(g)).astype(o_ref.dtype)


def _rope(y, cos, sins, first_half):
    partner = jnp.where(first_half, pltpu.roll(y, 96, 1), pltpu.roll(y, 32, 1))
    return y * cos + partner * sins


def _proj_gqa_kernel(x_ref, nw_ref, sh_ref, sc_ref, w_ref, qn_ref, kn_ref, cos_ref, sin_ref, o_ref, h_scr):
    _norm_mod_to_scratch(x_ref, nw_ref, sh_ref, sc_ref, h_scr)
    j = pl.program_id(1)
    y = _dot(h_scr[...], w_ref[...])
    tn = y.shape[1]
    n_q = GQA_HEADS * HEAD_DIM // tn
    n_k = GQA_KV_HEADS * HEAD_DIM // tn

    def normed(nw, scale):
        cos = cos_ref[...]
        sins = sin_ref[...]
        lane = lax.broadcasted_iota(jnp.int32, cos.shape, 1)
        first_half = (lane & 63) < 32
        for s in range(tn // HEAD_DIM):
            ys = y[:, s * HEAD_DIM:(s + 1) * HEAD_DIM]
            ys = _rms(ys, nw) * scale
            o_ref[:, s * HEAD_DIM:(s + 1) * HEAD_DIM] = _rope(ys, cos, sins, first_half).astype(o_ref.dtype)

    @pl.when(j < n_q)
    def _():
        normed(qn_ref[...], HEAD_DIM ** -0.5)

    @pl.when(jnp.logical_and(j >= n_q, j < n_q + n_k))
    def _():
        normed(kn_ref[...], 1.0)

    @pl.when(j >= n_q + n_k)
    def _():
        o_ref[...] = y.astype(o_ref.dtype)


def _project(x, nw, sh, sc, w, *, mode, tm, tn, out_dtype=BF16, bias=None, n_scaled=0, scale=1.0,
             extra=None, name):
    rows = x.shape[0]
    n = w.shape[1]
    nt = rows // tm
    mi = _mod_index(tm)
    row_spec = pl.BlockSpec((tm, D_MODEL), lambda i, j: (i, 0))
    vec_spec = pl.BlockSpec((1, D_MODEL), lambda i, j: (0, 0))
    mod_spec = pl.BlockSpec((None, 1, D_MODEL), lambda i, j: (mi(i), 0, 0))
    common = [row_spec, vec_spec, mod_spec, mod_spec]
    args = [x, nw, sh, sc]
    if mode == "plain":
        n_out = n
        if bias is None:
            bias = jnp.zeros((1, n), F32)
        in_specs = common + [pl.BlockSpec((D_MODEL, tn), lambda i, j: (0, j)),
                             pl.BlockSpec((1, tn), lambda i, j: (0, j))]
        args += [w, bias]
        body = functools.partial(_proj_plain_kernel, n_scaled=n_scaled, scale=scale)
    elif mode == "glu":
        n_out = n // 2
        off = n_out // tn
        in_specs = common + [pl.BlockSpec((D_MODEL, tn), lambda i, j: (0, j)),
                             pl.BlockSpec((D_MODEL, tn), lambda i, j: (0, j + off)),
                             pl.BlockSpec((1, tn), lambda i, j: (0, j)),
                             pl.BlockSpec((1, tn), lambda i, j: (0, j + off))]
        args += [w, w, bias, bias]
        body = _proj_glu_kernel
    else:
        n_out = n
        qn, kn, cos, sins = extra
        head_spec = pl.BlockSpec((1, HEAD_DIM), lambda i, j: (0, 0))
        tab_spec = pl.BlockSpec((tm, HEAD_DIM), lambda i, j: (i, 0))
        in_specs = common + [pl.BlockSpec((D_MODEL, tn), lambda i, j: (0, j)),
                             head_spec, head_spec, tab_spec, tab_spec]
        args += [w, qn, kn, cos, sins]
        body = _proj_gqa_kernel
    return pl.pallas_call(
        body,
        grid=(nt, n_out // tn),
        in_specs=in_specs,
        out_specs=pl.BlockSpec((tm, tn), lambda i, j: (i, j)),
        out_shape=jax.ShapeDtypeStruct((rows, n_out), out_dtype),
        scratch_shapes=[pltpu.VMEM((tm, D_MODEL), BF16)],
        compiler_params=_cparams(("parallel", "arbitrary")),
        name=name,
    )(*args)


def _outproj_kernel(a_ref, w_ref, b_ref, x_ref, pw_ref, g_ref, o_ref):
    y = _dot(a_ref[...], w_ref[...]) + b_ref[...]
    o_ref[...] = x_ref[...] + g_ref[...] * _rms(y, pw_ref[...])


def _outproj_mlstm_kernel(hf_ref, hb_ref, og_ref, on_ref, w_ref, b_ref, x_ref, pw_ref, g_ref, o_ref, a_scr):
    for h in range(MLSTM_HEADS):
        sl = slice(h * MLSTM_DV, (h + 1) * MLSTM_DV)
        hh = hf_ref[:, sl] + hb_ref[:, sl]
        hh = _rms(hh, on_ref[:, sl]) * _sigmoid(og_ref[:, sl].astype(F32))
        a_scr[:, sl] = hh.astype(BF16)
    y = _dot(a_scr[...], w_ref[...]) + b_ref[...]
    o_ref[...] = x_ref[...] + g_ref[...] * _rms(y, pw_ref[...])


def _out_project(a, w, bias, x, pw, g, *, tm, rows_out, name, mlstm=None):
    nt = rows_out // tm
    mi = _mod_index(tm)
    row_spec = pl.BlockSpec((tm, D_MODEL), lambda i: (i, 0))
    vec_spec = pl.BlockSpec((1, D_MODEL), lambda i: (0, 0))
    w_spec = pl.BlockSpec((D_MODEL, D_MODEL), lambda i: (0, 0))
    mod_spec = pl.BlockSpec((None, 1, D_MODEL), lambda i: (mi(i), 0, 0))
    if bias is None:
        bias = jnp.zeros((1, D_MODEL), F32)
    if mlstm is None:
        body = _outproj_kernel
        in_specs = [row_spec, w_spec, vec_spec, row_spec, vec_spec, mod_spec]
        args = [a, w, bias, x, pw, g]
        scratch = []
    else:
        hdir, z, out_norm = mlstm
        body = _outproj_mlstm_kernel
        o_gate_block = (2 * MLSTM_HEADS * MLSTM_DQK + MLSTM_HEADS * MLSTM_DV) // D_MODEL
        in_specs = [pl.BlockSpec((None, tm, D_MODEL), lambda i: (0, i, 0)),
                    pl.BlockSpec((None, tm, D_MODEL), lambda i: (1, i, 0)),
                    pl.BlockSpec((tm, D_MODEL), lambda i: (i, o_gate_block)),
                    vec_spec, w_spec, vec_spec, row_spec, vec_spec, mod_spec]
        args = [hdir, hdir, z, out_norm, w, bias, x, pw, g]
        scratch = [pltpu.VMEM((tm, D_MODEL), BF16)]
    return pl.pallas_call(
        body,
        grid=(nt,),
        in_specs=in_specs,
        out_specs=row_spec,
        out_shape=jax.ShapeDtypeStruct((rows_out, D_MODEL), F32),
        scratch_shapes=scratch,
        compiler_params=_cparams(("parallel",)),
        name=name,
    )(*args)


def _ffn_kernel(x_ref, xp_ref, xn_ref, nw_ref, sh_ref, sc_ref, g_ref, wv_ref, wg_ref, cwv_ref, cwg_ref,
                cbv_ref, cbg_ref, wd_ref, pw_ref, o_ref, h_scr, uv_scr, ug_scr, acc_scr, *, tm):
    i = pl.program_id(0)
    f = pl.program_id(1)

    @pl.when(f == 0)
    def _():
        def nm(xr):
            return (_rms(xr[...], nw_ref[...]) * (1.0 + sc_ref[...]) + sh_ref[...]).astype(BF16)
        h_scr[0:HALO, :] = nm(xp_ref)
        h_scr[HALO:HALO + tm, :] = nm(x_ref)
        h_scr[HALO + tm:HALO + tm + HALO, :] = nm(xn_ref)
        acc_scr[...] = jnp.zeros_like(acc_scr)

    h = h_scr[...]
    uv_scr[...] = _dot(h, wv_ref[...])
    ug_scr[...] = _dot(h, wg_ref[...])
    r = i * tm + lax.broadcasted_iota(jnp.int32, (tm, 1), 0)
    has_left = jnp.where(_is_seq_start(r), 0.0, 1.0)
    has_right = jnp.where(_is_seq_start(r + 1), 0.0, 1.0)

    def conv(u_scr, cw_ref, cb_ref):
        left = u_scr[HALO - 1:HALO - 1 + tm, :] * has_left
        right = u_scr[HALO + 1:HALO + 1 + tm, :] * has_right
        return (left * cw_ref[0:1, :] + u_scr[HALO:HALO + tm, :] * cw_ref[1:2, :]
                + right * cw_ref[2:3, :] + cb_ref[...])

    val = conv(uv_scr, cwv_ref, cbv_ref)
    gate = conv(ug_scr, cwg_ref, cbg_ref)
    act = (val * (gate * _sigmoid(gate))).astype(BF16)
    acc_scr[...] += _dot(act, wd_ref[...])

    @pl.when(f == pl.num_programs(1) - 1)
    def _():
        o_ref[...] = x_ref[...] + g_ref[...] * _rms(acc_scr[...], pw_ref[...])


def _conv_ffn(x, nw, sh, sc, g, w_up, conv_w, conv_b, w_down, pw, *, tm, tf, rows_out, name):
    rows = x.shape[0]
    nt = rows_out // tm
    nf = D_FF // tf
    mi = _mod_index(tm)
    hb = tm // HALO
    last_halo = rows // HALO - 1
    vec_spec = pl.BlockSpec((1, D_MODEL), lambda i, f: (0, 0))
    mod_spec = pl.BlockSpec((None, 1, D_MODEL), lambda i, f: (mi(i), 0, 0))
    in_specs = [
        pl.BlockSpec((tm, D_MODEL), lambda i, f: (i, 0)),
        pl.BlockSpec((HALO, D_MODEL), lambda i, f: (jnp.maximum(i * hb - 1, 0), 0)),
        pl.BlockSpec((HALO, D_MODEL), lambda i, f: (jnp.minimum((i + 1) * hb, last_halo), 0)),
        vec_spec, mod_spec, mod_spec, mod_spec,
        pl.BlockSpec((D_MODEL, tf), lambda i, f: (0, f)),
        pl.BlockSpec((D_MODEL, tf), lambda i, f: (0, f + nf)),
        pl.BlockSpec((3, tf), lambda i, f: (0, f)),
        pl.BlockSpec((3, tf), lambda i, f: (0, f + nf)),
        pl.BlockSpec((1, tf), lambda i, f: (0, f)),
        pl.BlockSpec((1, tf), lambda i, f: (0, f + nf)),
        pl.BlockSpec((tf, D_MODEL), lambda i, f: (f, 0)),
        vec_spec,
    ]
    return pl.pallas_call(
        functools.partial(_ffn_kernel, tm=tm),
        grid=(nt, nf),
        in_specs=in_specs,
        out_specs=pl.BlockSpec((tm, D_MODEL), lambda i, f: (i, 0)),
        out_shape=jax.ShapeDtypeStruct((rows_out, D_MODEL), F32),
        scratch_shapes=[pltpu.VMEM((tm + 2 * HALO, D_MODEL), BF16),
                        pltpu.VMEM((tm + 2 * HALO, tf), F32),
                        pltpu.VMEM((tm + 2 * HALO, tf), F32),
                        pltpu.VMEM((tm, D_MODEL), F32)],
        compiler_params=_cparams(("parallel", "arbitrary")),
        name=name,
    )(x, x, x, nw, sh, sc, g, w_up, w_up, conv_w, conv_w, conv_b, conv_b, w_down, pw)


def _gqa_attn_kernel(q_ref, kl_ref, vl_ref, kc_ref, vc_ref, o_ref):
    t = pl.program_id(2)
    n_lat_tiles = pl.num_programs(2) - 1
    group = GQA_HEADS // GQA_KV_HEADS
    kc = kc_ref[...]
    vc = vc_ref[...]

    @pl.when(t < n_lat_tiles)
    def _():
        kl = kl_ref[...]
        vl = vl_ref[...]
        for gi in range(group):
            sl = slice(gi * HEAD_DIM, (gi + 1) * HEAD_DIM)
            q = q_ref[:, sl]
            s_c = _dot_nt(q, kc)
            s_l = _dot_nt(q, kl)
            m = jnp.maximum(jnp.max(s_c, axis=1, keepdims=True), jnp.max(s_l, axis=1, keepdims=True))
            p_c = jnp.exp(s_c - m)
            p_l = jnp.exp(s_l - m)
            den = jnp.sum(p_c, axis=1, keepdims=True) + jnp.sum(p_l, axis=1, keepdims=True)
            o = _dot(p_c.astype(BF16), vc) + _dot(p_l.astype(BF16), vl)
            o_ref[:, sl] = (o / den).astype(o_ref.dtype)

    @pl.when(t == n_lat_tiles)
    def _():
        for gi in range(group):
            sl = slice(gi * HEAD_DIM, (gi + 1) * HEAD_DIM)
            q = q_ref[:, sl]
            s_c = _dot_nt(q, kc)
            p_c = jnp.exp(s_c - jnp.max(s_c, axis=1, keepdims=True))
            o = _dot(p_c.astype(BF16), vc)
            o_ref[:, sl] = (o / jnp.sum(p_c, axis=1, keepdims=True)).astype(o_ref.dtype)


def _gqa_attention(z):
    tq = SEG
    n_lat = SEQ // tq
    group_w = (GQA_HEADS // GQA_KV_HEADS) * HEAD_DIM
    k_col = GQA_HEADS
    v_col = GQA_HEADS + GQA_KV_HEADS
    ctx_blk = R_LAT // SEG

    def q_rows(b, kh, t):
        return jnp.where(t < n_lat, b * n_lat + t, ctx_blk + b)

    return pl.pallas_call(
        _gqa_attn_kernel,
        grid=(BATCH, GQA_KV_HEADS, n_lat + 1),
        in_specs=[
            pl.BlockSpec((tq, group_w), lambda b, kh, t: (q_rows(b, kh, t), kh)),
            pl.BlockSpec((SEQ, HEAD_DIM), lambda b, kh, t: (b, k_col + kh)),
            pl.BlockSpec((SEQ, HEAD_DIM), lambda b, kh, t: (b, v_col + kh)),
            pl.BlockSpec((CTX_LEN, HEAD_DIM), lambda b, kh, t: (ctx_blk + b, k_col + kh)),
            pl.BlockSpec((CTX_LEN, HEAD_DIM), lambda b, kh, t: (ctx_blk + b, v_col + kh)),
        ],
        out_specs=pl.BlockSpec((tq, group_w), lambda b, kh, t: (q_rows(b, kh, t), kh)),
        out_shape=jax.ShapeDtypeStruct((R_ALL, GQA_HEADS * HEAD_DIM), BF16),
        compiler_params=_cparams(("parallel", "parallel", "arbitrary")),
        name="gqa_attention",
    )(z, z, z, z, z)


def _split3(x):
    hi = x.astype(BF16)
    r1 = x - hi.astype(F32)
    mid = r1.astype(BF16)
    lo = (r1 - mid.astype(F32)).astype(BF16)
    return hi, mid, lo


def _log_sigmoid(x):
    return jnp.minimum(x, 0.0) - jnp.log(1.0 + jnp.exp(-jnp.abs(x)))


def _mlstm_kernel(q_ref, k_ref, v_ref, gc_ref, gr_ref, o_ref, c_scr, n_scr, m_scr):
    d = pl.program_id(1)
    c = pl.program_id(2)
    nh = MLSTM_HEADS
    L = SEG
    scale = MLSTM_DQK ** -0.5

    @pl.when(c == 0)
    def _():
        c_scr[...] = jnp.zeros_like(c_scr)
        n_scr[...] = jnp.zeros_like(n_scr)
        m_scr[...] = jnp.full_like(m_scr, M_INIT)

    ri = lax.broadcasted_iota(jnp.int32, (L, L), 0)
    ci = lax.broadcasted_iota(jnp.int32, (L, L), 1)
    diff = jnp.where(d == 0, ci - ri, ri - ci)
    sees = diff <= 0
    cum = jnp.where(sees, 1.0, 0.0).astype(BF16)
    cum_t = jnp.where(diff >= 0, 1.0, 0.0).astype(BF16)

    gc = gc_ref[...]
    gr = gr_ref[...]
    lf_c = _log_sigmoid(gc)
    lf_r = _log_sigmoid(gr)
    b_c = sum(_dot(cum, p) for p in _split3(lf_c))
    b_r = sum(_dot(p, cum_t) for p in _split3(lf_r))
    g_all = jnp.sum(lf_c, axis=0, keepdims=True)

    for h in range(nh):
        ig_c = gc[:, h:h + 1]
        bc = b_c[:, nh + h:nh + h + 1]
        ig_r = gr[h:h + 1, :]
        br = b_r[nh + h:nh + h + 1, :]
        g = g_all[:, nh + h:nh + h + 1]
        m_in = m_scr[h:h + 1, 0:1]
        a_r = g - br + ig_r
        a_c = g - bc + ig_c
        m_loc = jnp.max(a_r, axis=1, keepdims=True)
        w_c = jnp.exp(a_c - m_loc)

        q = q_ref[:, h * MLSTM_DQK:(h + 1) * MLSTM_DQK]
        k = k_ref[:, h * MLSTM_DQK:(h + 1) * MLSTM_DQK]
        v = v_ref[:, h * MLSTM_DV:(h + 1) * MLSTM_DV]

        log_d = jnp.where(sees, bc - br + ig_r, NEG)
        log_inter = bc + m_in
        m_j = jnp.maximum(log_inter, jnp.max(log_d, axis=1, keepdims=True))
        w_intra = jnp.exp(log_d - m_j)
        w_inter = jnp.exp(log_inter - m_j) * scale
        s = _dot_nt(q, k) * (w_intra * scale)
        c_in = c_scr[h]
        n_in = n_scr[h:h + 1, :]
        num = _dot(s.astype(BF16), v) + w_inter * _dot(q, c_in.astype(BF16))
        den = (jnp.sum(s, axis=1, keepdims=True)
               + w_inter * jnp.sum(q.astype(F32) * n_in, axis=1, keepdims=True))
        o_ref[:, h * MLSTM_DV:(h + 1) * MLSTM_DV] = num / jnp.maximum(jnp.abs(den), jnp.exp(-m_j))

        kw = k.astype(F32) * w_c
        kv = _dot_tn(kw.astype(BF16), v)
        ks = jnp.sum(kw, axis=0, keepdims=True)
        m_new = jnp.maximum(g + m_in, m_loc)
        a_old = jnp.exp(g + m_in - m_new)
        a_new = jnp.exp(m_loc - m_new)
        c_scr[h] = a_old * c_in + a_new * kv
        n_scr[h:h + 1, :] = a_old * n_in + a_new * ks
        m_scr[h:h + 1, :] = jnp.broadcast_to(m_new, (1, 128))


def _mlstm(z, gates_c, gates_r):
    n_lat = SEQ // SEG
    ctx_blk = R_LAT // SEG
    nqk = MLSTM_HEADS * MLSTM_DQK
    nv = MLSTM_HEADS * MLSTM_DV

    def rb(b, d, c):
        lat = jnp.where(d == 0, c - 1, n_lat - c)
        return jnp.where(c == 0, ctx_blk + b, b * n_lat + lat)

    return pl.pallas_call(
        _mlstm_kernel,
        grid=(BATCH, 2, n_lat + 1),
        in_specs=[
            pl.BlockSpec((SEG, nqk), lambda b, d, c: (rb(b, d, c), 0)),
            pl.BlockSpec((SEG, nqk), lambda b, d, c: (rb(b, d, c), 1)),
            pl.BlockSpec((SEG, nv), lambda b, d, c: (rb(b, d, c), 2 * nqk // nv)),
            pl.BlockSpec((None, SEG, 128), lambda b, d, c: (d, rb(b, d, c), 0)),
            pl.BlockSpec((None, 2 * MLSTM_HEADS, SEG), lambda b, d, c: (d, 0, rb(b, d, c))),
        ],
        out_specs=pl.BlockSpec((None, SEG, nv), lambda b, d, c: (d, rb(b, d, c), 0)),
        out_shape=jax.ShapeDtypeStruct((2, R_ALL, nv), F32),
        scratch_shapes=[pltpu.VMEM((MLSTM_HEADS, MLSTM_DQK, MLSTM_DV), F32),
                        pltpu.VMEM((MLSTM_HEADS, MLSTM_DQK), F32),
                        pltpu.VMEM((MLSTM_HEADS, 128), F32)],
        compiler_params=_cparams(("parallel", "parallel", "arbitrary")),
        name="mlstm",
    )(z, z, z, gates_c, gates_r)


def _dwconv_ln_kernel(u_ref, up_ref, un_ref, w_ref, b_ref, lw_ref, lb_ref, o_ref, pad_scr, conv_scr):
    i = pl.program_id(0)
    tm = u_ref.shape[0]
    r0 = i * tm
    keep_prev = jnp.logical_not(_is_seq_start(r0))
    keep_next = jnp.logical_not(_is_seq_start(r0 + tm))
    pad_scr[0:HALO, :] = jnp.where(keep_prev, up_ref[...].astype(F32), 0.0)
    pad_scr[HALO:HALO + tm, :] = u_ref[...].astype(F32)
    pad_scr[HALO + tm:HALO + tm + HALO, :] = jnp.where(keep_next, un_ref[...].astype(F32), 0.0)

    rc = 32
    cw = 512
    half = CONV_WIDTH // 2

    def chunk(cs, carry):
        cl = pl.ds(pl.multiple_of(cs * cw, cw), cw)
        for ib in range(tm // rc):
            base = ib * rc
            acc = jnp.zeros((rc, cw), F32) + b_ref[:, cl]
            for kk in range(CONV_WIDTH):
                off = base + HALO - half + kk
                acc = acc + pad_scr[off:off + rc, cl] * w_ref[kk:kk + 1, cl]
            conv_scr[base:base + rc, cl] = acc
        return carry

    lax.fori_loop(0, D_MODEL // cw, chunk, 0)
    y = conv_scr[...]
    mu = jnp.mean(y, axis=-1, keepdims=True)
    yc = y - mu
    var = jnp.mean(yc * yc, axis=-1, keepdims=True)
    yn = yc * lax.rsqrt(var + EPS) * lw_ref[...] + lb_ref[...]
    o_ref[...] = (yn * _sigmoid(yn)).astype(o_ref.dtype)


def _dwconv_ln_swish(u, w_dw, b_dw, ln_w, ln_b):
    rows = u.shape[0]
    tm = SEG
    hb = tm // HALO
    last_halo = rows // HALO - 1
    vec_spec = pl.BlockSpec((1, D_MODEL), lambda i: (0, 0))
    return pl.pallas_call(
        _dwconv_ln_kernel,
        grid=(rows // tm,),
        in_specs=[
            pl.BlockSpec((tm, D_MODEL), lambda i: (i, 0)),
            pl.BlockSpec((HALO, D_MODEL), lambda i: (jnp.maximum(i * hb - 1, 0), 0)),
            pl.BlockSpec((HALO, D_MODEL), lambda i: (jnp.minimum((i + 1) * hb, last_halo), 0)),
            pl.BlockSpec((CONV_WIDTH, D_MODEL), lambda i: (0, 0)),
            vec_spec, vec_spec, vec_spec,
        ],
        out_specs=pl.BlockSpec((tm, D_MODEL), lambda i: (i, 0)),
        out_shape=jax.ShapeDtypeStruct((rows, D_MODEL), BF16),
        scratch_shapes=[pltpu.VMEM((tm + 2 * HALO, D_MODEL), F32),
                        pltpu.VMEM((tm, D_MODEL), F32)],
        compiler_params=_cparams(("parallel",)),
        name="dwconv_ln_swish",
    )(u, u, u, w_dw, b_dw, ln_w, ln_b)


NAT_Q_ROWS = 4
NAT_BAND_ROWS = 12
NAT_TQ = NAT_Q_ROWS * GRID_W
NAT_BAND = NAT_BAND_ROWS * GRID_W


def _nat_kernel(q_ref, kl_ref, vl_ref, kc_ref, vc_ref, bias_ref, o_ref):
    gq = pl.program_id(2)
    n_groups = pl.num_programs(2)
    max_start = (SEQ - NAT_BAND) // NAT_TQ
    start = pl.multiple_of(jnp.clip(gq - 1, 0, max_start) * NAT_TQ, NAT_TQ)
    del n_groups
    q = q_ref[...]
    kb = kl_ref[pl.ds(start, NAT_BAND), :]
    vb = vl_ref[pl.ds(start, NAT_BAND), :]
    s_n = _dot_nt(q, kb) + bias_ref[...]
    s_c = _dot_nt(q, kc_ref[...])
    m = jnp.maximum(jnp.max(s_n, axis=1, keepdims=True), jnp.max(s_c, axis=1, keepdims=True))
    p_n = jnp.exp(s_n - m)
    p_c = jnp.exp(s_c - m)
    den = jnp.sum(p_n, axis=1, keepdims=True) + jnp.sum(p_c, axis=1, keepdims=True)
    o = _dot(p_n.astype(BF16), vb) + _dot(p_c.astype(BF16), vc_ref[...])
    o_ref[...] = (o / den).astype(o_ref.dtype)


def _nat_bias_index():
    rows = SEQ // GRID_W
    n_groups = rows // NAT_Q_ROWS
    idx = np.zeros((3, NAT_TQ, NAT_BAND), np.int32)
    valid = np.zeros((3, NAT_TQ, NAT_BAND), bool)
    for var, gq in enumerate((0, 1, n_groups - 1)):
        lo = int(np.clip(gq - 1, 0, (SEQ - NAT_BAND) // NAT_TQ)) * NAT_Q_ROWS
        q_row = gq * NAT_Q_ROWS + np.arange(NAT_TQ) // GRID_W
        q_col = np.arange(NAT_TQ) % GRID_W
        k_row = lo + np.arange(NAT_BAND) // GRID_W
        k_col = np.arange(NAT_BAND) % GRID_W
        r_start = np.clip(q_row - NA_ROWS // 2, 0, rows - NA_ROWS)
        c_start = np.clip(q_col - NA_COLS // 2, 0, GRID_W - NA_COLS)
        dr = k_row[None, :] - r_start[:, None]
        dc = k_col[None, :] - c_start[:, None]
        ok = (dr >= 0) & (dr < NA_ROWS) & (dc >= 0) & (dc < NA_COLS)
        ridx = k_row[None, :] - q_row[:, None] + (NA_ROWS - 1)
        cidx = k_col[None, :] - q_col[:, None] + (NA_COLS - 1)
        flat = ridx * (2 * NA_COLS - 1) + cidx
        idx[var] = np.where(ok, flat, 0)
        valid[var] = ok
    return idx, valid


def _nat_attention(z, rpb):
    n_groups = SEQ // NAT_TQ
    ctx_blk = R_LAT // SEG
    idx, valid = _nat_bias_index()
    table = rpb.reshape(NA_HEADS, -1)
    bias = jnp.where(valid[None], jnp.take(table, idx.reshape(-1), axis=1).reshape(NA_HEADS, *idx.shape), NEG)

    def variant(gq):
        return (gq > 0).astype(jnp.int32) + (gq == n_groups - 1).astype(jnp.int32)

    k_col = NA_HEADS
    v_col = 2 * NA_HEADS
    return pl.pallas_call(
        _nat_kernel,
        grid=(BATCH, NA_HEADS, n_groups),
        in_specs=[
            pl.BlockSpec((NAT_TQ, NA_HEAD_DIM), lambda b, h, gq: (b * n_groups + gq, h)),
            pl.BlockSpec((SEQ, NA_HEAD_DIM), lambda b, h, gq: (b, k_col + h)),
            pl.BlockSpec((SEQ, NA_HEAD_DIM), lambda b, h, gq: (b, v_col + h)),
            pl.BlockSpec((CTX_LEN, NA_HEAD_DIM), lambda b, h, gq: (ctx_blk + b, k_col + h)),
            pl.BlockSpec((CTX_LEN, NA_HEAD_DIM), lambda b, h, gq: (ctx_blk + b, v_col + h)),
            pl.BlockSpec((None, None, NAT_TQ, NAT_BAND), lambda b, h, gq: (h, variant(gq), 0, 0)),
        ],
        out_specs=pl.BlockSpec((NAT_TQ, NA_HEAD_DIM), lambda b, h, gq: (b * n_groups + gq, h)),
        out_shape=jax.ShapeDtypeStruct((R_LAT, NA_HEADS * NA_HEAD_DIM), BF16),
        compiler_params=_cparams(("parallel", "parallel", "arbitrary")),
        name="nat_attention",
    )(z, z, z, z, z, bias)


def _rope_tables():
    pos = jnp.arange(SEQ)
    quarter = HEAD_DIM // 4
    freqs = ROPE_THETA ** (-jnp.arange(quarter, dtype=F32) / quarter)

    def cs(p):
        ang = p.astype(F32)[:, None] * freqs[None, :]
        return jnp.cos(ang), jnp.sin(ang)

    cr, sr = cs(pos // GRID_W)
    cc, sc = cs(pos % GRID_W)
    cos = jnp.concatenate([cr, cr, cc, cc], axis=1)
    sins = jnp.concatenate([-sr, sr, -sc, sc], axis=1)
    n_ctx = BATCH * CTX_LEN
    cos = jnp.concatenate([cos] * BATCH + [jnp.ones((n_ctx, HEAD_DIM), F32)], axis=0)
    sins = jnp.concatenate([sins] * BATCH + [jnp.zeros((n_ctx, HEAD_DIM), F32)], axis=0)
    return cos, sins


def kernel(x, c, ctx, c_ctx, mod_w, mod_b, norm_pre_mix, norm_post_mix, norm_pre_ffn, norm_post_ffn, ffn_w_up, ffn_conv_w, ffn_conv_b, ffn_w_down, gqa_w_in, gqa_q_norm, gqa_k_norm, gqa_w_out, mlstm_w_in, mlstm_b_gate, mlstm_out_norm, mlstm_w_out, conv_w_pw1, conv_b_pw1, conv_w_dw, conv_b_dw, conv_ln_w, conv_ln_b, conv_w_pw2, conv_b_pw2, nat_w_in, nat_rpb, nat_w_out):
    tm = 512
    xs = jnp.concatenate([x.reshape(R_LAT, D_MODEL), ctx.reshape(BATCH * CTX_LEN, D_MODEL)], axis=0)
    a8 = jnp.concatenate([c, c_ctx[None, :], jnp.zeros((8 - BATCH - 1, D_MODEL), F32)], axis=0)
    mods = _modulation(a8, mod_w, mod_b)
    mods = mods[:, :BATCH + 1].reshape(DEPTH, BATCH + 1, MOD_CHUNKS, 1, D_MODEL)
    cos, sins = _rope_tables()

    def row(v):
        return v.reshape(1, -1)

    for i in range(DEPTH):
        kind = i % 4
        last = i == DEPTH - 1
        rows_out = R_LAT if last else R_ALL
        sh1, sc1, g1, sh2, sc2, g2 = [mods[i, :, m] for m in range(MOD_CHUNKS)]
        pre = row(norm_pre_mix[i])
        post = row(norm_post_mix[i])
        if kind == 0:
            z = _project(xs, pre, sh1, sc1, gqa_w_in[0].astype(BF16), mode="gqa", tm=tm, tn=512,
                         extra=(row(gqa_q_norm[0]), row(gqa_k_norm[0]), cos, sins), name="gqa_in")
            a = _gqa_attention(z)
            xs = _out_project(a, gqa_w_out[0].astype(BF16), None, xs, post, g1, tm=tm, rows_out=rows_out,
                              name="gqa_out")
        elif kind == 1:
            n_main = 2 * MLSTM_HEADS * MLSTM_DQK + 2 * MLSTM_HEADS * MLSTM_DV
            n_gate = 4 * MLSTM_HEADS
            w_in = mlstm_w_in[0]
            z = _project(xs, pre, sh1, sc1, w_in[:, :n_main].astype(BF16), mode="plain", tm=tm, tn=512,
                         name="mlstm_in")
            w_g = jnp.pad(w_in[:, n_main:], ((0, 0), (0, 128 - n_gate))).astype(BF16)
            b_g = jnp.pad(mlstm_b_gate[0], (0, 128 - n_gate)).reshape(1, 128)
            gates = _project(xs, pre, sh1, sc1, w_g, mode="plain", tm=tm, tn=128, out_dtype=F32, bias=b_g,
                             name="mlstm_gates")
            gd = gates[:, :n_gate].reshape(R_ALL, 2, 2 * MLSTM_HEADS).transpose(1, 0, 2)
            gates_c = jnp.pad(gd, ((0, 0), (0, 0), (0, 128 - 2 * MLSTM_HEADS)))
            gates_r = gd.transpose(0, 2, 1)
            hdir = _mlstm(z, gates_c, gates_r)
            xs = _out_project(None, mlstm_w_out[0].astype(BF16), None, xs, post, g1, tm=SEG, rows_out=rows_out,
                              name="mlstm_out", mlstm=(hdir, z, row(mlstm_out_norm[0])))
        elif kind == 2:
            u = _project(xs, pre, sh1, sc1, conv_w_pw1[0].astype(BF16), mode="glu", tm=tm, tn=512,
                         bias=row(conv_b_pw1[0]), name="conformer_pw1")
            a = _dwconv_ln_swish(u, conv_w_dw[0], row(conv_b_dw[0]), row(conv_ln_w[0]), row(conv_ln_b[0]))
            xs = _out_project(a, conv_w_pw2[0].astype(BF16), row(conv_b_pw2[0]), xs, post, g1, tm=tm,
                              rows_out=rows_out, name="conformer_pw2")
        else:
            nq = NA_HEADS * NA_HEAD_DIM
            z = _project(xs, pre, sh1, sc1, nat_w_in[0].astype(BF16), mode="plain", tm=tm, tn=512,
                         n_scaled=nq // 512, scale=NA_HEAD_DIM ** -0.5, name="nat_in")
            a = _nat_attention(z, nat_rpb[0])
            xs = _out_project(a, nat_w_out[0].astype(BF16), None, xs, post, g1, tm=tm, rows_out=rows_out,
                              name="nat_out")
        xs = _conv_ffn(xs, row(norm_pre_ffn[i]), sh2, sc2, g2, ffn_w_up[i].astype(BF16), ffn_conv_w[i],
                       row(ffn_conv_b[i]), ffn_w_down[i].astype(BF16), row(norm_post_ffn[i]),
                       tm=tm, tf=512, rows_out=rows_out, name="conv_ffn")
    return xs.reshape(BATCH, SEQ, D_MODEL)
```

```python
import functools

import numpy as np
import jax
import jax.numpy as jnp
from jax import lax
from jax.experimental import pallas as pl
from jax.experimental.pallas import tpu as pltpu

F32 = jnp.float32
BF16 = jnp.bfloat16

D_MODEL = 2048
BATCH = 2
SEQ = 4096
DEPTH = 4
GRID_W = 64
CTX_LEN = 256
EPS = 1e-6
MOD_CHUNKS = 6
D_FF = 5632
GQA_HEADS = 16
GQA_KV_HEADS = 4
HEAD_DIM = 128
ROPE_THETA = 10000.0
MLSTM_HEADS = 8
MLSTM_DQK = 128
MLSTM_DV = 256
M_INIT = -1e30
CONV_WIDTH = 31
NA_HEADS = 16
NA_HEAD_DIM = 128
NA_ROWS = 8
NA_COLS = 16

R_LAT = BATCH * SEQ
R_ALL = R_LAT + BATCH * CTX_LEN
NEG = -1e30

VMEM_LIMIT = 56 * 1024 * 1024
SEG = 256
HALO = 16
FFN_SUB = 256


def _cparams(sem, flags=None):
    return pltpu.CompilerParams(dimension_semantics=sem, vmem_limit_bytes=VMEM_LIMIT, flags=flags)


def _sigmoid(x):
    return 1.0 / (1.0 + jnp.exp(-x))


def _rms(x, w):
    return x * lax.rsqrt(jnp.mean(x * x, axis=-1, keepdims=True) + EPS) * w


def _dot(a, b):
    return jnp.dot(a, b, preferred_element_type=F32)


def _dot_nt(a, b):
    return lax.dot_general(a, b, (((1,), (1,)), ((), ())), preferred_element_type=F32)


def _dot_tn(a, b):
    return lax.dot_general(a, b, (((0,), (0,)), ((), ())), preferred_element_type=F32)


def _is_seq_start(r):
    return (r & jnp.where(r < R_LAT, SEQ - 1, CTX_LEN - 1)) == 0


def _mod_index(tm):
    per_batch = SEQ // tm
    return lambda i: jnp.minimum(i // per_batch, BATCH)


def _mod_kernel(a_ref, w_ref, b_ref, o_ref):
    a = a_ref[...]
    s = (a * _sigmoid(a)).astype(BF16)
    o_ref[...] = _dot(s, w_ref[...].astype(BF16)) + b_ref[...]


def _modulation(a8, mod_w, mod_b):
    tn = 1024
    n = MOD_CHUNKS * D_MODEL
    return pl.pallas_call(
        _mod_kernel,
        grid=(DEPTH, n // tn),
        in_specs=[
            pl.BlockSpec((8, D_MODEL), lambda l, j: (0, 0)),
            pl.BlockSpec((None, D_MODEL, tn), lambda l, j: (l, 0, j)),
            pl.BlockSpec((None, 1, tn), lambda l, j: (l, 0, j)),
        ],
        out_specs=pl.BlockSpec((None, 8, tn), lambda l, j: (l, 0, j)),
        out_shape=jax.ShapeDtypeStruct((DEPTH, 8, n), F32),
        compiler_params=_cparams(("parallel", "parallel")),
        name="modulation",
    )(a8, mod_w, mod_b.reshape(DEPTH, 1, n))


def _norm_mod_to_scratch(x_ref, nw_ref, sh_ref, sc_ref, h_scr):
    @pl.when(pl.program_id(1) == 0)
    def _():
        h = _rms(x_ref[...], nw_ref[...]) * (1.0 + sc_ref[...]) + sh_ref[...]
        h_scr[...] = h.astype(BF16)


def _proj_plain_kernel(x_ref, nw_ref, sh_ref, sc_ref, w_ref, b_ref, o_ref, h_scr, *, n_scaled, scale):
    _norm_mod_to_scratch(x_ref, nw_ref, sh_ref, sc_ref, h_scr)
    y = _dot(h_scr[...], w_ref[...]) + b_ref[...]
    if n_scaled:
        y = y * jnp.where(pl.program_id(1) < n_scaled, scale, 1.0).astype(F32)
    o_ref[...] = y.astype(o_ref.dtype)


def _proj_glu_kernel(x_ref, nw_ref, sh_ref, sc_ref, wa_ref, wg_ref, ba_ref, bg_ref, o_ref, h_scr):
    _norm_mod_to_scratch(x_ref, nw_ref, sh_ref, sc_ref, h_scr)
    h = h_scr[...]
    a = _dot(h, wa_ref[...]) + ba_ref[...]
    g = _dot(h, wg_ref[...]) + bg_ref[...]
    o_ref[...] = (a * _sigmoid(g)).astype(o_ref.dtype)


def _rope(y, cos, sins, first_half):
    partner = jnp.where(first_half, pltpu.roll(y, 96, 1), pltpu.roll(y, 32, 1))
    return y * cos + partner * sins


def _proj_gqa_kernel(x_ref, nw_ref, sh_ref, sc_ref, w_ref, qn_ref, kn_ref, cos_ref, sin_ref, o_ref, h_scr):
    _norm_mod_to_scratch(x_ref, nw_ref, sh_ref, sc_ref, h_scr)
    j = pl.program_id(1)
    y = _dot(h_scr[...], w_ref[...])
    tn = y.shape[1]
    n_q = GQA_HEADS * HEAD_DIM // tn
    n_k = GQA_KV_HEADS * HEAD_DIM // tn

    def normed(nw, scale):
        cos = cos_ref[...]
        sins = sin_ref[...]
        lane = lax.broadcasted_iota(jnp.int32, cos.shape, 1)
        first_half = (lane & 63) < 32
        for s in range(tn // HEAD_DIM):
            ys = y[:, s * HEAD_DIM:(s + 1) * HEAD_DIM]
            ys = _rms(ys, nw) * scale
            o_ref[:, s * HEAD_DIM:(s + 1) * HEAD_DIM] = _rope(ys, cos, sins, first_half).astype(o_ref.dtype)

    @pl.when(j < n_q)
    def _():
        normed(qn_ref[...], HEAD_DIM ** -0.5)

    @pl.when(jnp.logical_and(j >= n_q, j < n_q + n_k))
    def _():
        normed(kn_ref[...], 1.0)

    @pl.when(j >= n_q + n_k)
    def _():
        o_ref[...] = y.astype(o_ref.dtype)


def _project(x, nw, sh, sc, w, *, mode, tm, tn, out_dtype=BF16, bias=None, n_scaled=0, scale=1.0,
             extra=None, name):
    rows = x.shape[0]
    n = w.shape[1]
    nt = rows // tm
    mi = _mod_index(tm)
    row_spec = pl.BlockSpec((tm, D_MODEL), lambda i, j: (i, 0))
    vec_spec = pl.BlockSpec((1, D_MODEL), lambda i, j: (0, 0))
    mod_spec = pl.BlockSpec((None, 1, D_MODEL), lambda i, j: (mi(i), 0, 0))
    common = [row_spec, vec_spec, mod_spec, mod_spec]
    args = [x, nw, sh, sc]
    if mode == "plain":
        n_out = n
        if bias is None:
            bias = jnp.zeros((1, n), F32)
        in_specs = common + [pl.BlockSpec((D_MODEL, tn), lambda i, j: (0, j)),
                             pl.BlockSpec((1, tn), lambda i, j: (0, j))]
        args += [w, bias]
        body = functools.partial(_proj_plain_kernel, n_scaled=n_scaled, scale=scale)
    elif mode == "glu":
        n_out = n // 2
        off = n_out // tn
        in_specs = common + [pl.BlockSpec((D_MODEL, tn), lambda i, j: (0, j)),
                             pl.BlockSpec((D_MODEL, tn), lambda i, j: (0, j + off)),
                             pl.BlockSpec((1, tn), lambda i, j: (0, j)),
                             pl.BlockSpec((1, tn), lambda i, j: (0, j + off))]
        args += [w, w, bias, bias]
        body = _proj_glu_kernel
    else:
        n_out = n
        qn, kn, cos, sins = extra
        head_spec = pl.BlockSpec((1, HEAD_DIM), lambda i, j: (0, 0))
        tab_spec = pl.BlockSpec((tm, HEAD_DIM), lambda i, j: (i, 0))
        in_specs = common + [pl.BlockSpec((D_MODEL, tn), lambda i, j: (0, j)),
                             head_spec, head_spec, tab_spec, tab_spec]
        args += [w, qn, kn, cos, sins]
        body = _proj_gqa_kernel
    return pl.pallas_call(
        body,
        grid=(nt, n_out // tn),
        in_specs=in_specs,
        out_specs=pl.BlockSpec((tm, tn), lambda i, j: (i, j)),
        out_shape=jax.ShapeDtypeStruct((rows, n_out), out_dtype),
        scratch_shapes=[pltpu.VMEM((tm, D_MODEL), BF16)],
        compiler_params=_cparams(("parallel", "arbitrary")),
        name=name,
    )(*args)


def _outproj_kernel(a_ref, w_ref, b_ref, x_ref, pw_ref, g_ref, o_ref):
    y = _dot(a_ref[...], w_ref[...]) + b_ref[...]
    o_ref[...] = x_ref[...] + g_ref[...] * _rms(y, pw_ref[...])


def _outproj_mlstm_kernel(hf_ref, hb_ref, og_ref, on_ref, w_ref, b_ref, x_ref, pw_ref, g_ref, o_ref, a_scr):
    for h in range(MLSTM_HEADS):
        sl = slice(h * MLSTM_DV, (h + 1) * MLSTM_DV)
        hh = hf_ref[:, sl] + hb_ref[:, sl]
        hh = _rms(hh, on_ref[:, sl]) * _sigmoid(og_ref[:, sl].astype(F32))
        a_scr[:, sl] = hh.astype(BF16)
    y = _dot(a_scr[...], w_ref[...]) + b_ref[...]
    o_ref[...] = x_ref[...] + g_ref[...] * _rms(y, pw_ref[...])


def _out_project(a, w, bias, x, pw, g, *, tm, rows_out, name, mlstm=None):
    nt = rows_out // tm
    mi = _mod_index(tm)
    row_spec = pl.BlockSpec((tm, D_MODEL), lambda i: (i, 0))
    vec_spec = pl.BlockSpec((1, D_MODEL), lambda i: (0, 0))
    w_spec = pl.BlockSpec((D_MODEL, D_MODEL), lambda i: (0, 0))
    mod_spec = pl.BlockSpec((None, 1, D_MODEL), lambda i: (mi(i), 0, 0))
    if bias is None:
        bias = jnp.zeros((1, D_MODEL), F32)
    if mlstm is None:
        body = _outproj_kernel
        in_specs = [row_spec, w_spec, vec_spec, row_spec, vec_spec, mod_spec]
        args = [a, w, bias, x, pw, g]
        scratch = []
    else:
        hdir, z, out_norm = mlstm
        body = _outproj_mlstm_kernel
        o_gate_block = (2 * MLSTM_HEADS * MLSTM_DQK + MLSTM_HEADS * MLSTM_DV) // D_MODEL
        in_specs = [pl.BlockSpec((None, tm, D_MODEL), lambda i: (0, i, 0)),
                    pl.BlockSpec((None, tm, D_MODEL), lambda i: (1, i, 0)),
                    pl.BlockSpec((tm, D_MODEL), lambda i: (i, o_gate_block)),
                    vec_spec, w_spec, vec_spec, row_spec, vec_spec, mod_spec]
        args = [hdir, hdir, z, out_norm, w, bias, x, pw, g]
        scratch = [pltpu.VMEM((tm, D_MODEL), BF16)]
    return pl.pallas_call(
        body,
        grid=(nt,),
        in_specs=in_specs,
        out_specs=row_spec,
        out_shape=jax.ShapeDtypeStruct((rows_out, D_MODEL), F32),
        scratch_shapes=scratch,
        compiler_params=_cparams(("parallel",)),
        name=name,
    )(*args)


def _ffn_kernel(x_ref, xp_ref, xn_ref, nw_ref, sh_ref, sc_ref, g_ref, wv_ref, wg_ref, cwv_ref, cwg_ref,
                cbv_ref, cbg_ref, wd_ref, pw_ref, o_ref, h_scr, uv_scr, ug_scr, acc_scr, *, tm):
    i = pl.program_id(0)
    f = pl.program_id(1)

    @pl.when(f == 0)
    def _():
        def nm(xr):
            return (_rms(xr[...], nw_ref[...]) * (1.0 + sc_ref[...]) + sh_ref[...]).astype(BF16)
        h_scr[0:HALO, :] = nm(xp_ref)
        h_scr[HALO:HALO + tm, :] = nm(x_ref)
        h_scr[HALO + tm:HALO + tm + HALO, :] = nm(xn_ref)
        acc_scr[...] = jnp.zeros_like(acc_scr)

    r = i * tm + lax.broadcasted_iota(jnp.int32, (tm, 1), 0)
    has_left = jnp.where(_is_seq_start(r), 0.0, 1.0)
    has_right = jnp.where(_is_seq_start(r + 1), 0.0, 1.0)

    def conv(u_scr, cw_ref, cb_ref, cl):
        left = u_scr[HALO - 1:HALO - 1 + tm, :] * has_left
        right = u_scr[HALO + 1:HALO + 1 + tm, :] * has_right
        return (left * cw_ref[0:1, cl] + u_scr[HALO:HALO + tm, :] * cw_ref[1:2, cl]
                + right * cw_ref[2:3, cl] + cb_ref[:, cl])

    h = h_scr[...]
    n_sub = uv_scr.shape[0]
    for cb in range(n_sub):
        cl = slice(cb * FFN_SUB, (cb + 1) * FFN_SUB)
        uv_scr[cb] = _dot(h, wv_ref[:, cl])
        ug_scr[cb] = _dot(h, wg_ref[:, cl])
    acts = []
    for cb in range(n_sub):
        cl = slice(cb * FFN_SUB, (cb + 1) * FFN_SUB)
        val = conv(uv_scr.at[cb], cwv_ref, cbv_ref, cl)
        gate = conv(ug_scr.at[cb], cwg_ref, cbg_ref, cl)
        acts.append((val * (gate * _sigmoid(gate))).astype(BF16))
    for cb in range(n_sub):
        acc_scr[...] += _dot(acts[cb], wd_ref[cb * FFN_SUB:(cb + 1) * FFN_SUB, :])

    @pl.when(f == pl.num_programs(1) - 1)
    def _():
        o_ref[...] = x_ref[...] + g_ref[...] * _rms(acc_scr[...], pw_ref[...])


def _conv_ffn(x, nw, sh, sc, g, w_up, conv_w, conv_b, w_down, pw, *, tm, tf, rows_out, name):
    rows = x.shape[0]
    nt = rows_out // tm
    nf = D_FF // tf
    mi = _mod_index(tm)
    hb = tm // HALO
    last_halo = rows // HALO - 1
    vec_spec = pl.BlockSpec((1, D_MODEL), lambda i, f: (0, 0))
    mod_spec = pl.BlockSpec((None, 1, D_MODEL), lambda i, f: (mi(i), 0, 0))
    in_specs = [
        pl.BlockSpec((tm, D_MODEL), lambda i, f: (i, 0)),
        pl.BlockSpec((HALO, D_MODEL), lambda i, f: (jnp.maximum(i * hb - 1, 0), 0)),
        pl.BlockSpec((HALO, D_MODEL), lambda i, f: (jnp.minimum((i + 1) * hb, last_halo), 0)),
        vec_spec, mod_spec, mod_spec, mod_spec,
        pl.BlockSpec((D_MODEL, tf), lambda i, f: (0, f)),
        pl.BlockSpec((D_MODEL, tf), lambda i, f: (0, f + nf)),
        pl.BlockSpec((3, tf), lambda i, f: (0, f)),
        pl.BlockSpec((3, tf), lambda i, f: (0, f + nf)),
        pl.BlockSpec((1, tf), lambda i, f: (0, f)),
        pl.BlockSpec((1, tf), lambda i, f: (0, f + nf)),
        pl.BlockSpec((tf, D_MODEL), lambda i, f: (f, 0)),
        vec_spec,
    ]
    return pl.pallas_call(
        functools.partial(_ffn_kernel, tm=tm),
        grid=(nt, nf),
        in_specs=in_specs,
        out_specs=pl.BlockSpec((tm, D_MODEL), lambda i, f: (i, 0)),
        out_shape=jax.ShapeDtypeStruct((rows_out, D_MODEL), F32),
        scratch_shapes=[pltpu.VMEM((tm + 2 * HALO, D_MODEL), BF16),
                        pltpu.VMEM((tf // FFN_SUB, tm + 2 * HALO, FFN_SUB), F32),
                        pltpu.VMEM((tf // FFN_SUB, tm + 2 * HALO, FFN_SUB), F32),
                        pltpu.VMEM((tm, D_MODEL), F32)],
        compiler_params=_cparams(("parallel", "arbitrary")),
        name=name,
    )(x, x, x, nw, sh, sc, g, w_up, w_up, conv_w, conv_w, conv_b, conv_b, w_down, pw)


def _gqa_attn_kernel(q_ref, kl_ref, vl_ref, kc_ref, vc_ref, o_ref):
    t = pl.program_id(2)
    n_lat_tiles = pl.num_programs(2) - 1
    group = GQA_HEADS // GQA_KV_HEADS
    kc = kc_ref[...]
    vc = vc_ref[...]

    @pl.when(t < n_lat_tiles)
    def _():
        kl = kl_ref[...]
        vl = vl_ref[...]
        for gi in range(group):
            sl = slice(gi * HEAD_DIM, (gi + 1) * HEAD_DIM)
            q = q_ref[:, sl]
            s_c = _dot_nt(q, kc)
            s_l = _dot_nt(q, kl)
            m = jnp.maximum(jnp.max(s_c, axis=1, keepdims=True), jnp.max(s_l, axis=1, keepdims=True))
            p_c = jnp.exp(s_c - m)
            p_l = jnp.exp(s_l - m)
            den = jnp.sum(p_c, axis=1, keepdims=True) + jnp.sum(p_l, axis=1, keepdims=True)
            o = _dot(p_c.astype(BF16), vc) + _dot(p_l.astype(BF16), vl)
            o_ref[:, sl] = (o / den).astype(o_ref.dtype)

    @pl.when(t == n_lat_tiles)
    def _():
        for gi in range(group):
            sl = slice(gi * HEAD_DIM, (gi + 1) * HEAD_DIM)
            q = q_ref[:, sl]
            s_c = _dot_nt(q, kc)
            p_c = jnp.exp(s_c - jnp.max(s_c, axis=1, keepdims=True))
            o = _dot(p_c.astype(BF16), vc)
            o_ref[:, sl] = (o / jnp.sum(p_c, axis=1, keepdims=True)).astype(o_ref.dtype)


def _gqa_attention(z):
    tq = SEG
    n_lat = SEQ // tq
    group_w = (GQA_HEADS // GQA_KV_HEADS) * HEAD_DIM
    k_col = GQA_HEADS
    v_col = GQA_HEADS + GQA_KV_HEADS
    ctx_blk = R_LAT // SEG

    def q_rows(b, kh, t):
        return jnp.where(t < n_lat, b * n_lat + t, ctx_blk + b)

    return pl.pallas_call(
        _gqa_attn_kernel,
        grid=(BATCH, GQA_KV_HEADS, n_lat + 1),
        in_specs=[
            pl.BlockSpec((tq, group_w), lambda b, kh, t: (q_rows(b, kh, t), kh)),
            pl.BlockSpec((SEQ, HEAD_DIM), lambda b, kh, t: (b, k_col + kh)),
            pl.BlockSpec((SEQ, HEAD_DIM), lambda b, kh, t: (b, v_col + kh)),
            pl.BlockSpec((CTX_LEN, HEAD_DIM), lambda b, kh, t: (ctx_blk + b, k_col + kh)),
            pl.BlockSpec((CTX_LEN, HEAD_DIM), lambda b, kh, t: (ctx_blk + b, v_col + kh)),
        ],
        out_specs=pl.BlockSpec((tq, group_w), lambda b, kh, t: (q_rows(b, kh, t), kh)),
        out_shape=jax.ShapeDtypeStruct((R_ALL, GQA_HEADS * HEAD_DIM), BF16),
        compiler_params=_cparams(("parallel", "parallel", "arbitrary")),
        name="gqa_attention",
    )(z, z, z, z, z)


def _split3(x):
    hi = x.astype(BF16)
    r1 = x - hi.astype(F32)
    mid = r1.astype(BF16)
    lo = (r1 - mid.astype(F32)).astype(BF16)
    return hi, mid, lo


def _log_sigmoid(x):
    return jnp.minimum(x, 0.0) - jnp.log(1.0 + jnp.exp(-jnp.abs(x)))


def _mlstm_kernel(q_ref, k_ref, v_ref, gc_ref, gr_ref, o_ref, c_scr, n_scr, m_scr):
    d = pl.program_id(1)
    c = pl.program_id(2)
    nh = MLSTM_HEADS
    L = SEG
    scale = MLSTM_DQK ** -0.5

    @pl.when(c == 0)
    def _():
        c_scr[...] = jnp.zeros_like(c_scr)
        n_scr[...] = jnp.zeros_like(n_scr)
        m_scr[...] = jnp.full_like(m_scr, M_INIT)

    ri = lax.broadcasted_iota(jnp.int32, (L, L), 0)
    ci = lax.broadcasted_iota(jnp.int32, (L, L), 1)
    diff = jnp.where(d == 0, ci - ri, ri - ci)
    sees = diff <= 0
    cum = jnp.where(sees, 1.0, 0.0).astype(BF16)
    cum_t = jnp.where(diff >= 0, 1.0, 0.0).astype(BF16)

    gc = gc_ref[...]
    gr = gr_ref[...]
    lf_c = _log_sigmoid(gc)
    lf_r = _log_sigmoid(gr)
    b_c = sum(_dot(cum, p) for p in _split3(lf_c))
    b_r = sum(_dot(p, cum_t) for p in _split3(lf_r))
    g_all = jnp.sum(lf_c, axis=0, keepdims=True)

    for h in range(nh):
        ig_c = gc[:, h:h + 1]
        bc = b_c[:, nh + h:nh + h + 1]
        ig_r = gr[h:h + 1, :]
        br = b_r[nh + h:nh + h + 1, :]
        g = g_all[:, nh + h:nh + h + 1]
        m_in = m_scr[h:h + 1, 0:1]
        a_r = g - br + ig_r
        a_c = g - bc + ig_c
        m_loc = jnp.max(a_r, axis=1, keepdims=True)
        w_c = jnp.exp(a_c - m_loc)

        q = q_ref[:, h * MLSTM_DQK:(h + 1) * MLSTM_DQK]
        k = k_ref[:, h * MLSTM_DQK:(h + 1) * MLSTM_DQK]
        v = v_ref[:, h * MLSTM_DV:(h + 1) * MLSTM_DV]

        log_d = jnp.where(sees, bc - br + ig_r, NEG)
        log_inter = bc + m_in
        m_j = jnp.maximum(log_inter, jnp.max(log_d, axis=1, keepdims=True))
        w_intra = jnp.exp(log_d - m_j)
        w_inter = jnp.exp(log_inter - m_j) * scale
        s = _dot_nt(q, k) * (w_intra * scale)
        c_in = c_scr[h]
        n_in = n_scr[h:h + 1, :]
        num = _dot(s.astype(BF16), v) + w_inter * _dot(q, c_in.astype(BF16))
        den = (jnp.sum(s, axis=1, keepdims=True)
               + w_inter * jnp.sum(q.astype(F32) * n_in, axis=1, keepdims=True))
        o_ref[:, h * MLSTM_DV:(h + 1) * MLSTM_DV] = num / jnp.maximum(jnp.abs(den), jnp.exp(-m_j))

        kw = k.astype(F32) * w_c
        kv = _dot_tn(kw.astype(BF16), v)
        ks = jnp.sum(kw, axis=0, keepdims=True)
        m_new = jnp.maximum(g + m_in, m_loc)
        a_old = jnp.exp(g + m_in - m_new)
        a_new = jnp.exp(m_loc - m_new)
        c_scr[h] = a_old * c_in + a_new * kv
        n_scr[h:h + 1, :] = a_old * n_in + a_new * ks
        m_scr[h:h + 1, :] = jnp.broadcast_to(m_new, (1, 128))


def _mlstm(z, gates_c, gates_r):
    n_lat = SEQ // SEG
    ctx_blk = R_LAT // SEG
    nqk = MLSTM_HEADS * MLSTM_DQK
    nv = MLSTM_HEADS * MLSTM_DV

    def rb(b, d, c):
        lat = jnp.where(d == 0, c - 1, n_lat - c)
        return jnp.where(c == 0, ctx_blk + b, b * n_lat + lat)

    return pl.pallas_call(
        _mlstm_kernel,
        grid=(BATCH, 2, n_lat + 1),
        in_specs=[
            pl.BlockSpec((SEG, nqk), lambda b, d, c: (rb(b, d, c), 0)),
            pl.BlockSpec((SEG, nqk), lambda b, d, c: (rb(b, d, c), 1)),
            pl.BlockSpec((SEG, nv), lambda b, d, c: (rb(b, d, c), 2 * nqk // nv)),
            pl.BlockSpec((None, SEG, 128), lambda b, d, c: (d, rb(b, d, c), 0)),
            pl.BlockSpec((None, 2 * MLSTM_HEADS, SEG), lambda b, d, c: (d, 0, rb(b, d, c))),
        ],
        out_specs=pl.BlockSpec((None, SEG, nv), lambda b, d, c: (d, rb(b, d, c), 0)),
        out_shape=jax.ShapeDtypeStruct((2, R_ALL, nv), F32),
        scratch_shapes=[pltpu.VMEM((MLSTM_HEADS, MLSTM_DQK, MLSTM_DV), F32),
                        pltpu.VMEM((MLSTM_HEADS, MLSTM_DQK), F32),
                        pltpu.VMEM((MLSTM_HEADS, 128), F32)],
        compiler_params=_cparams(("parallel", "parallel", "arbitrary")),
        name="mlstm",
    )(z, z, z, gates_c, gates_r)


DW_ROWS = 32
DW_COLS = 512


def _dwconv_ln_kernel(u_ref, up_ref, un_ref, w_ref, b_ref, lw_ref, lb_ref, o_ref, pad_scr, conv_scr, sh_scr):
    i = pl.program_id(0)
    tm = u_ref.shape[0]
    r0 = i * tm
    keep_prev = jnp.logical_not(_is_seq_start(r0))
    keep_next = jnp.logical_not(_is_seq_start(r0 + tm))
    pad_scr[0:HALO, :] = jnp.where(keep_prev, up_ref[...].astype(F32), 0.0)
    pad_scr[HALO:HALO + tm, :] = u_ref[...].astype(F32)
    pad_scr[HALO + tm:HALO + tm + HALO, :] = jnp.where(keep_next, un_ref[...].astype(F32), 0.0)

    rc = DW_ROWS
    cw = DW_COLS
    half = CONV_WIDTH // 2
    n_sh = sh_scr.shape[1]

    def chunk(cs, carry):
        cl = pl.ds(pl.multiple_of(cs * cw, cw), cw)
        for s in range(1, 8):
            sh_scr[s] = pad_scr[s:s + n_sh, cl]
        for ib in range(tm // rc):
            base = ib * rc
            acc = jnp.zeros((rc, cw), F32) + b_ref[:, cl]
            for kk in range(CONV_WIDTH):
                al, s = divmod(HALO - half + kk, 8)
                off = base + 8 * al
                tap = pad_scr[off:off + rc, cl] if s == 0 else sh_scr[s, off:off + rc, :]
                acc = acc + tap * w_ref[kk:kk + 1, cl]
            conv_scr[base:base + rc, cl] = acc
        return carry

    lax.fori_loop(0, D_MODEL // cw, chunk, 0)
    y = conv_scr[...]
    mu = jnp.mean(y, axis=-1, keepdims=True)
    yc = y - mu
    var = jnp.mean(yc * yc, axis=-1, keepdims=True)
    yn = yc * lax.rsqrt(var + EPS) * lw_ref[...] + lb_ref[...]
    o_ref[...] = (yn * _sigmoid(yn)).astype(o_ref.dtype)


def _dwconv_ln_swish(u, w_dw, b_dw, ln_w, ln_b):
    rows = u.shape[0]
    tm = SEG
    hb = tm // HALO
    last_halo = rows // HALO - 1
    vec_spec = pl.BlockSpec((1, D_MODEL), lambda i: (0, 0))
    return pl.pallas_call(
        _dwconv_ln_kernel,
        grid=(rows // tm,),
        in_specs=[
            pl.BlockSpec((tm, D_MODEL), lambda i: (i, 0)),
            pl.BlockSpec((HALO, D_MODEL), lambda i: (jnp.maximum(i * hb - 1, 0), 0)),
            pl.BlockSpec((HALO, D_MODEL), lambda i: (jnp.minimum((i + 1) * hb, last_halo), 0)),
            pl.BlockSpec((CONV_WIDTH, D_MODEL), lambda i: (0, 0)),
            vec_spec, vec_spec, vec_spec,
        ],
        out_specs=pl.BlockSpec((tm, D_MODEL), lambda i: (i, 0)),
        out_shape=jax.ShapeDtypeStruct((rows, D_MODEL), BF16),
        scratch_shapes=[pltpu.VMEM((tm + 2 * HALO, D_MODEL), F32),
                        pltpu.VMEM((tm, D_MODEL), F32),
                        pltpu.VMEM((8, tm + 2 * HALO - 8, DW_COLS), F32)],
        compiler_params=_cparams(("parallel",)),
        name="dwconv_ln_swish",
    )(u, u, u, w_dw, b_dw, ln_w, ln_b)


NAT_Q_ROWS = 4
NAT_BAND_ROWS = 12
NAT_TQ = NAT_Q_ROWS * GRID_W
NAT_BAND = NAT_BAND_ROWS * GRID_W


NAT_HEADS_PER_STEP = 4


def _nat_kernel(q_ref, kl_ref, vl_ref, kc_ref, vc_ref, bias_ref, o_ref):
    gq = pl.program_id(2)
    max_start = (SEQ - NAT_BAND) // NAT_TQ
    start = pl.multiple_of(jnp.clip(gq - 1, 0, max_start) * NAT_TQ, NAT_TQ)
    for hh in range(NAT_HEADS_PER_STEP):
        sl = slice(hh * NA_HEAD_DIM, (hh + 1) * NA_HEAD_DIM)
        q = q_ref[:, sl]
        kb = kl_ref[pl.ds(start, NAT_BAND), sl]
        vb = vl_ref[pl.ds(start, NAT_BAND), sl]
        s_n = _dot_nt(q, kb) + bias_ref[hh]
        s_c = _dot_nt(q, kc_ref[:, sl])
        m = jnp.maximum(jnp.max(s_n, axis=1, keepdims=True), jnp.max(s_c, axis=1, keepdims=True))
        p_n = jnp.exp(s_n - m)
        p_c = jnp.exp(s_c - m)
        den = jnp.sum(p_n, axis=1, keepdims=True) + jnp.sum(p_c, axis=1, keepdims=True)
        o = _dot(p_n.astype(BF16), vb) + _dot(p_c.astype(BF16), vc_ref[:, sl])
        o_ref[:, sl] = (o / den).astype(o_ref.dtype)


def _nat_bias_blocks():
    rows = SEQ // GRID_W
    n_groups = rows // NAT_Q_ROWS
    q_col = np.arange(GRID_W)
    c_start = np.clip(q_col - NA_COLS // 2, 0, GRID_W - NA_COLS)
    dc = q_col[None, :] - c_start[:, None]
    col_ok = (dc >= 0) & (dc < NA_COLS)
    masked = 2 * NA_ROWS - 1
    block = np.full((3, NAT_Q_ROWS, NAT_BAND_ROWS), masked, np.int64)
    for var, gq in enumerate((0, 1, n_groups - 1)):
        lo = int(np.clip(gq - 1, 0, (SEQ - NAT_BAND) // NAT_TQ)) * NAT_Q_ROWS
        for qr in range(NAT_Q_ROWS):
            q_row = gq * NAT_Q_ROWS + qr
            r_start = int(np.clip(q_row - NA_ROWS // 2, 0, rows - NA_ROWS))
            for kr in range(NAT_BAND_ROWS):
                k_row = lo + kr
                if 0 <= k_row - r_start < NA_ROWS:
                    block[var, qr, kr] = k_row - q_row + (NA_ROWS - 1)
    return col_ok, block


def _nat_bias(rpb):
    n_r, n_c = 2 * NA_ROWS - 1, 2 * NA_COLS - 1
    w = GRID_W
    col_ok, block = _nat_bias_blocks()
    lead = w - NA_COLS
    vext = jnp.pad(rpb, ((0, 0), (0, 0), (lead, 2 * w - lead - n_c)))
    skew = jnp.broadcast_to(vext[:, :, None, :], (NA_HEADS, n_r, w, 2 * w)).reshape(NA_HEADS, n_r, 2 * w * w)
    skew = skew[:, :, :w * (2 * w - 1)].reshape(NA_HEADS, n_r, w, 2 * w - 1)
    tb = jnp.where(col_ok[None, None], skew[..., w - 1:], NEG)
    tb = jnp.concatenate([tb, jnp.full((NA_HEADS, 1, w, w), NEG, F32)], axis=1)
    blocks = jnp.stack([tb[:, int(d)] for d in block.reshape(-1)], axis=1)
    blocks = blocks.reshape(NA_HEADS, 3, NAT_Q_ROWS, NAT_BAND_ROWS, w, w)
    return blocks.transpose(0, 1, 2, 4, 3, 5).reshape(NA_HEADS, 3, NAT_TQ, NAT_BAND)


def _nat_attention(z, rpb):
    n_groups = SEQ // NAT_TQ
    ctx_blk = R_LAT // SEG
    bias = _nat_bias(rpb)
    hps = NAT_HEADS_PER_STEP
    gw = hps * NA_HEAD_DIM

    def variant(gq):
        return (gq > 0).astype(jnp.int32) + (gq == n_groups - 1).astype(jnp.int32)

    k_col = NA_HEADS // hps
    v_col = 2 * NA_HEADS // hps
    return pl.pallas_call(
        _nat_kernel,
        grid=(BATCH, NA_HEADS // hps, n_groups),
        in_specs=[
            pl.BlockSpec((NAT_TQ, gw), lambda b, h, gq: (b * n_groups + gq, h)),
            pl.BlockSpec((SEQ, gw), lambda b, h, gq: (b, k_col + h)),
            pl.BlockSpec((SEQ, gw), lambda b, h, gq: (b, v_col + h)),
            pl.BlockSpec((CTX_LEN, gw), lambda b, h, gq: (ctx_blk + b, k_col + h)),
            pl.BlockSpec((CTX_LEN, gw), lambda b, h, gq: (ctx_blk + b, v_col + h)),
            pl.BlockSpec((hps, None, NAT_TQ, NAT_BAND), lambda b, h, gq: (h, variant(gq), 0, 0)),
        ],
        out_specs=pl.BlockSpec((NAT_TQ, gw), lambda b, h, gq: (b * n_groups + gq, h)),
        out_shape=jax.ShapeDtypeStruct((R_LAT, NA_HEADS * NA_HEAD_DIM), BF16),
        compiler_params=_cparams(("parallel", "parallel", "arbitrary")),
        name="nat_attention",
    )(z, z, z, z, z, bias)


def _rope_tables():
    pos = jnp.arange(SEQ)
    quarter = HEAD_DIM // 4
    freqs = ROPE_THETA ** (-jnp.arange(quarter, dtype=F32) / quarter)

    def cs(p):
        ang = p.astype(F32)[:, None] * freqs[None, :]
        return jnp.cos(ang), jnp.sin(ang)

    cr, sr = cs(pos // GRID_W)
    cc, sc = cs(pos % GRID_W)
    cos = jnp.concatenate([cr, cr, cc, cc], axis=1)
    sins = jnp.concatenate([-sr, sr, -sc, sc], axis=1)
    n_ctx = BATCH * CTX_LEN
    cos = jnp.concatenate([cos] * BATCH + [jnp.ones((n_ctx, HEAD_DIM), F32)], axis=0)
    sins = jnp.concatenate([sins] * BATCH + [jnp.zeros((n_ctx, HEAD_DIM), F32)], axis=0)
    return cos, sins


def kernel(x, c, ctx, c_ctx, mod_w, mod_b, norm_pre_mix, norm_post_mix, norm_pre_ffn, norm_post_ffn, ffn_w_up, ffn_conv_w, ffn_conv_b, ffn_w_down, gqa_w_in, gqa_q_norm, gqa_k_norm, gqa_w_out, mlstm_w_in, mlstm_b_gate, mlstm_out_norm, mlstm_w_out, conv_w_pw1, conv_b_pw1, conv_w_dw, conv_b_dw, conv_ln_w, conv_ln_b, conv_w_pw2, conv_b_pw2, nat_w_in, nat_rpb, nat_w_out):
    tm = 512
    xs = jnp.concatenate([x.reshape(R_LAT, D_MODEL), ctx.reshape(BATCH * CTX_LEN, D_MODEL)], axis=0)
    a8 = jnp.concatenate([c, c_ctx[None, :], jnp.zeros((8 - BATCH - 1, D_MODEL), F32)], axis=0)
    mods = _modulation(a8, mod_w, mod_b)
    mods = mods[:, :BATCH + 1].reshape(DEPTH, BATCH + 1, MOD_CHUNKS, 1, D_MODEL)
    cos, sins = _rope_tables()

    def row(v):
        return v.reshape(1, -1)

    for i in range(DEPTH):
        kind = i % 4
        last = i == DEPTH - 1
        rows_out = R_LAT if last else R_ALL
        sh1, sc1, g1, sh2, sc2, g2 = [mods[i, :, m] for m in range(MOD_CHUNKS)]
        pre = row(norm_pre_mix[i])
        post = row(norm_post_mix[i])
        if kind == 0:
            z = _project(xs, pre, sh1, sc1, gqa_w_in[0].astype(BF16), mode="gqa", tm=tm, tn=512,
                         extra=(row(gqa_q_norm[0]), row(gqa_k_norm[0]), cos, sins), name="gqa_in")
            a = _gqa_attention(z)
            xs = _out_project(a, gqa_w_out[0].astype(BF16), None, xs, post, g1, tm=tm, rows_out=rows_out,
                              name="gqa_out")
        elif kind == 1:
            n_main = 2 * MLSTM_HEADS * MLSTM_DQK + 2 * MLSTM_HEADS * MLSTM_DV
            n_gate = 4 * MLSTM_HEADS
            w_in = mlstm_w_in[0]
            z = _project(xs, pre, sh1, sc1, w_in[:, :n_main].astype(BF16), mode="plain", tm=tm, tn=2048,
                         name="mlstm_in")
            w_g = jnp.pad(w_in[:, n_main:], ((0, 0), (0, 128 - n_gate))).astype(BF16)
            b_g = jnp.pad(mlstm_b_gate[0], (0, 128 - n_gate)).reshape(1, 128)
            gates = _project(xs, pre, sh1, sc1, w_g, mode="plain", tm=tm, tn=128, out_dtype=F32, bias=b_g,
                             name="mlstm_gates")
            gd = gates[:, :n_gate].reshape(R_ALL, 2, 2 * MLSTM_HEADS).transpose(1, 0, 2)
            gates_c = jnp.pad(gd, ((0, 0), (0, 0), (0, 128 - 2 * MLSTM_HEADS)))
            gates_r = gd.transpose(0, 2, 1)
            hdir = _mlstm(z, gates_c, gates_r)
            xs = _out_project(None, mlstm_w_out[0].astype(BF16), None, xs, post, g1, tm=SEG, rows_out=rows_out,
                              name="mlstm_out", mlstm=(hdir, z, row(mlstm_out_norm[0])))
        elif kind == 2:
            u = _project(xs, pre, sh1, sc1, conv_w_pw1[0].astype(BF16), mode="glu", tm=tm, tn=1024,
                         bias=row(conv_b_pw1[0]), name="conformer_pw1")
            a = _dwconv_ln_swish(u, conv_w_dw[0], row(conv_b_dw[0]), row(conv_ln_w[0]), row(conv_ln_b[0]))
            xs = _out_project(a, conv_w_pw2[0].astype(BF16), row(conv_b_pw2[0]), xs, post, g1, tm=tm,
                              rows_out=rows_out, name="conformer_pw2")
        else:
            nq = NA_HEADS * NA_HEAD_DIM
            z = _project(xs, pre, sh1, sc1, nat_w_in[0].astype(BF16), mode="plain", tm=tm, tn=2048,
                         n_scaled=nq // 2048, scale=NA_HEAD_DIM ** -0.5, name="nat_in")
            a = _nat_attention(z, nat_rpb[0])
            xs = _out_project(a, nat_w_out[0].astype(BF16), None, xs, post, g1, tm=tm, rows_out=rows_out,
                              name="nat_out")
        xs = _conv_ffn(xs, row(norm_pre_ffn[i]), sh2, sc2, g2, ffn_w_up[i].astype(BF16), ffn_conv_w[i],
                       row(ffn_conv_b[i]), ffn_w_down[i].astype(BF16), row(norm_post_ffn[i]),
                       tm=tm, tf=512, rows_out=rows_out, name="conv_ffn")
    return xs.reshape(BATCH, SEQ, D_MODEL)
```

```python
import functools

import numpy as np
import jax
import jax.numpy as jnp
from jax import lax
from jax.experimental import pallas as pl
from jax.experimental.pallas import tpu as pltpu

F32 = jnp.float32
BF16 = jnp.bfloat16

D_MODEL = 2048
BATCH = 2
SEQ = 4096
DEPTH = 4
GRID_W = 64
CTX_LEN = 256
EPS = 1e-6
MOD_CHUNKS = 6
D_FF = 5632
GQA_HEADS = 16
GQA_KV_HEADS = 4
HEAD_DIM = 128
ROPE_THETA = 10000.0
MLSTM_HEADS = 8
MLSTM_DQK = 128
MLSTM_DV = 256
M_INIT = -1e30
CONV_WIDTH = 31
NA_HEADS = 16
NA_HEAD_DIM = 128
NA_ROWS = 8
NA_COLS = 16

R_LAT = BATCH * SEQ
R_ALL = R_LAT + BATCH * CTX_LEN
NEG = -1e30

VMEM_LIMIT = 56 * 1024 * 1024
SEG = 256
HALO = 16
FFN_SUB = 256
FFN_ROW_BLOCKS = 2


def _cparams(sem, flags=None):
    return pltpu.CompilerParams(dimension_semantics=sem, vmem_limit_bytes=VMEM_LIMIT, flags=flags)


def _sigmoid(x):
    return 1.0 / (1.0 + jnp.exp(-x))


def _rms(x, w):
    return x * lax.rsqrt(jnp.mean(x * x, axis=-1, keepdims=True) + EPS) * w


def _dot(a, b):
    return jnp.dot(a, b, preferred_element_type=F32)


def _dot_nt(a, b):
    return lax.dot_general(a, b, (((1,), (1,)), ((), ())), preferred_element_type=F32)


def _dot_tn(a, b):
    return lax.dot_general(a, b, (((0,), (0,)), ((), ())), preferred_element_type=F32)


def _is_seq_start(r):
    return (r & jnp.where(r < R_LAT, SEQ - 1, CTX_LEN - 1)) == 0


def _mod_index(tm):
    per_batch = SEQ // tm
    return lambda i: jnp.minimum(i // per_batch, BATCH)


def _mod_kernel(a_ref, w_ref, b_ref, o_ref):
    a = a_ref[...]
    s = (a * _sigmoid(a)).astype(BF16)
    o_ref[...] = _dot(s, w_ref[...].astype(BF16)) + b_ref[...]


def _modulation(a8, mod_w, mod_b):
    tn = 1024
    n = MOD_CHUNKS * D_MODEL
    return pl.pallas_call(
        _mod_kernel,
        grid=(DEPTH, n // tn),
        in_specs=[
            pl.BlockSpec((8, D_MODEL), lambda l, j: (0, 0)),
            pl.BlockSpec((None, D_MODEL, tn), lambda l, j: (l, 0, j)),
            pl.BlockSpec((None, 1, tn), lambda l, j: (l, 0, j)),
        ],
        out_specs=pl.BlockSpec((None, 8, tn), lambda l, j: (l, 0, j)),
        out_shape=jax.ShapeDtypeStruct((DEPTH, 8, n), F32),
        compiler_params=_cparams(("parallel", "parallel")),
        name="modulation",
    )(a8, mod_w, mod_b.reshape(DEPTH, 1, n))


def _norm_mod_to_scratch(x_ref, nw_ref, sh_ref, sc_ref, h_scr):
    @pl.when(pl.program_id(1) == 0)
    def _():
        h = _rms(x_ref[...], nw_ref[...]) * (1.0 + sc_ref[...]) + sh_ref[...]
        h_scr[...] = h.astype(BF16)


def _proj_plain_kernel(x_ref, nw_ref, sh_ref, sc_ref, w_ref, b_ref, o_ref, h_scr, *, n_scaled, scale):
    _norm_mod_to_scratch(x_ref, nw_ref, sh_ref, sc_ref, h_scr)
    y = _dot(h_scr[...], w_ref[...]) + b_ref[...]
    if n_scaled:
        y = y * jnp.where(pl.program_id(1) < n_scaled, scale, 1.0).astype(F32)
    o_ref[...] = y.astype(o_ref.dtype)


def _proj_mlstm_kernel(x_ref, nw_ref, sh_ref, sc_ref, w_ref, wg_ref, bg_ref, o_ref, gt_ref, h_scr):
    _norm_mod_to_scratch(x_ref, nw_ref, sh_ref, sc_ref, h_scr)

    @pl.when(pl.program_id(1) == 0)
    def _():
        gt_ref[...] = (_dot(h_scr[...], wg_ref[...]) + bg_ref[...]).T

    o_ref[...] = _dot(h_scr[...], w_ref[...]).astype(o_ref.dtype)


def _proj_glu_kernel(x_ref, nw_ref, sh_ref, sc_ref, wa_ref, wg_ref, ba_ref, bg_ref, o_ref, h_scr):
    _norm_mod_to_scratch(x_ref, nw_ref, sh_ref, sc_ref, h_scr)
    h = h_scr[...]
    a = _dot(h, wa_ref[...]) + ba_ref[...]
    g = _dot(h, wg_ref[...]) + bg_ref[...]
    o_ref[...] = (a * _sigmoid(g)).astype(o_ref.dtype)


def _rope(y, cos, sins, first_half):
    partner = jnp.where(first_half, pltpu.roll(y, 96, 1), pltpu.roll(y, 32, 1))
    return y * cos + partner * sins


def _proj_gqa_kernel(x_ref, nw_ref, sh_ref, sc_ref, w_ref, qn_ref, kn_ref, cos_ref, sin_ref, o_ref, h_scr):
    _norm_mod_to_scratch(x_ref, nw_ref, sh_ref, sc_ref, h_scr)
    j = pl.program_id(1)
    y = _dot(h_scr[...], w_ref[...])
    tn = y.shape[1]
    n_q = GQA_HEADS * HEAD_DIM // tn
    n_k = GQA_KV_HEADS * HEAD_DIM // tn

    def normed(nw, scale):
        cos = cos_ref[...]
        sins = sin_ref[...]
        lane = lax.broadcasted_iota(jnp.int32, cos.shape, 1)
        first_half = (lane & 63) < 32
        for s in range(tn // HEAD_DIM):
            ys = y[:, s * HEAD_DIM:(s + 1) * HEAD_DIM]
            ys = _rms(ys, nw) * scale
            o_ref[:, s * HEAD_DIM:(s + 1) * HEAD_DIM] = _rope(ys, cos, sins, first_half).astype(o_ref.dtype)

    @pl.when(j < n_q)
    def _():
        normed(qn_ref[...], HEAD_DIM ** -0.5)

    @pl.when(jnp.logical_and(j >= n_q, j < n_q + n_k))
    def _():
        normed(kn_ref[...], 1.0)

    @pl.when(j >= n_q + n_k)
    def _():
        o_ref[...] = y.astype(o_ref.dtype)


def _project(x, nw, sh, sc, w, *, mode, tm, tn, out_dtype=BF16, bias=None, n_scaled=0, scale=1.0,
             extra=None, name):
    rows = x.shape[0]
    n = w.shape[1]
    nt = rows // tm
    mi = _mod_index(tm)
    row_spec = pl.BlockSpec((tm, D_MODEL), lambda i, j: (i, 0))
    vec_spec = pl.BlockSpec((1, D_MODEL), lambda i, j: (0, 0))
    mod_spec = pl.BlockSpec((None, 1, D_MODEL), lambda i, j: (mi(i), 0, 0))
    common = [row_spec, vec_spec, mod_spec, mod_spec]
    args = [x, nw, sh, sc]
    if mode == "plain":
        n_out = n
        if bias is None:
            bias = jnp.zeros((1, n), F32)
        in_specs = common + [pl.BlockSpec((D_MODEL, tn), lambda i, j: (0, j)),
                             pl.BlockSpec((1, tn), lambda i, j: (0, j))]
        args += [w, bias]
        body = functools.partial(_proj_plain_kernel, n_scaled=n_scaled, scale=scale)
    elif mode == "glu":
        n_out = n // 2
        off = n_out // tn
        in_specs = common + [pl.BlockSpec((D_MODEL, tn), lambda i, j: (0, j)),
                             pl.BlockSpec((D_MODEL, tn), lambda i, j: (0, j + off)),
                             pl.BlockSpec((1, tn), lambda i, j: (0, j)),
                             pl.BlockSpec((1, tn), lambda i, j: (0, j + off))]
        args += [w, w, bias, bias]
        body = _proj_glu_kernel
    elif mode == "mlstm":
        n_out = n
        w_g, b_g = extra
        ng = w_g.shape[1]
        in_specs = common + [pl.BlockSpec((D_MODEL, tn), lambda i, j: (0, j)),
                             pl.BlockSpec((D_MODEL, ng), lambda i, j: (0, 0)),
                             pl.BlockSpec((1, ng), lambda i, j: (0, 0))]
        args += [w, w_g, b_g]
        return pl.pallas_call(
            _proj_mlstm_kernel,
            grid=(nt, n_out // tn),
            in_specs=in_specs,
            out_specs=[pl.BlockSpec((tm, tn), lambda i, j: (i, j)),
                       pl.BlockSpec((ng, tm), lambda i, j: (0, i))],
            out_shape=[jax.ShapeDtypeStruct((rows, n_out), out_dtype),
                       jax.ShapeDtypeStruct((ng, rows), F32)],
            scratch_shapes=[pltpu.VMEM((tm, D_MODEL), BF16)],
            compiler_params=_cparams(("parallel", "arbitrary")),
            name=name,
        )(*args)
    else:
        n_out = n
        qn, kn, cos, sins = extra
        head_spec = pl.BlockSpec((1, HEAD_DIM), lambda i, j: (0, 0))
        tab_spec = pl.BlockSpec((tm, HEAD_DIM), lambda i, j: (i, 0))
        in_specs = common + [pl.BlockSpec((D_MODEL, tn), lambda i, j: (0, j)),
                             head_spec, head_spec, tab_spec, tab_spec]
        args += [w, qn, kn, cos, sins]
        body = _proj_gqa_kernel
    return pl.pallas_call(
        body,
        grid=(nt, n_out // tn),
        in_specs=in_specs,
        out_specs=pl.BlockSpec((tm, tn), lambda i, j: (i, j)),
        out_shape=jax.ShapeDtypeStruct((rows, n_out), out_dtype),
        scratch_shapes=[pltpu.VMEM((tm, D_MODEL), BF16)],
        compiler_params=_cparams(("parallel", "arbitrary")),
        name=name,
    )(*args)


def _outproj_kernel(a_ref, w_ref, b_ref, x_ref, pw_ref, g_ref, o_ref):
    y = _dot(a_ref[...], w_ref[...]) + b_ref[...]
    o_ref[...] = x_ref[...] + g_ref[...] * _rms(y, pw_ref[...])


def _outproj_mlstm_kernel(hf_ref, hb_ref, og_ref, on_ref, w_ref, b_ref, x_ref, pw_ref, g_ref, o_ref, a_scr):
    for h in range(MLSTM_HEADS):
        sl = slice(h * MLSTM_DV, (h + 1) * MLSTM_DV)
        hh = hf_ref[:, sl] + hb_ref[:, sl]
        hh = _rms(hh, on_ref[:, sl]) * _sigmoid(og_ref[:, sl].astype(F32))
        a_scr[:, sl] = hh.astype(BF16)
    y = _dot(a_scr[...], w_ref[...]) + b_ref[...]
    o_ref[...] = x_ref[...] + g_ref[...] * _rms(y, pw_ref[...])


def _out_project(a, w, bias, x, pw, g, *, tm, rows_out, name, mlstm=None):
    nt = rows_out // tm
    mi = _mod_index(tm)
    row_spec = pl.BlockSpec((tm, D_MODEL), lambda i: (i, 0))
    vec_spec = pl.BlockSpec((1, D_MODEL), lambda i: (0, 0))
    w_spec = pl.BlockSpec((D_MODEL, D_MODEL), lambda i: (0, 0))
    mod_spec = pl.BlockSpec((None, 1, D_MODEL), lambda i: (mi(i), 0, 0))
    if bias is None:
        bias = jnp.zeros((1, D_MODEL), F32)
    if mlstm is None:
        body = _outproj_kernel
        in_specs = [row_spec, w_spec, vec_spec, row_spec, vec_spec, mod_spec]
        args = [a, w, bias, x, pw, g]
        scratch = []
    else:
        hdir, z, out_norm = mlstm
        body = _outproj_mlstm_kernel
        o_gate_block = (2 * MLSTM_HEADS * MLSTM_DQK + MLSTM_HEADS * MLSTM_DV) // D_MODEL
        in_specs = [pl.BlockSpec((None, tm, D_MODEL), lambda i: (0, i, 0)),
                    pl.BlockSpec((None, tm, D_MODEL), lambda i: (1, i, 0)),
                    pl.BlockSpec((tm, D_MODEL), lambda i: (i, o_gate_block)),
                    vec_spec, w_spec, vec_spec, row_spec, vec_spec, mod_spec]
        args = [hdir, hdir, z, out_norm, w, bias, x, pw, g]
        scratch = [pltpu.VMEM((tm, D_MODEL), BF16)]
    return pl.pallas_call(
        body,
        grid=(nt,),
        in_specs=in_specs,
        out_specs=row_spec,
        out_shape=jax.ShapeDtypeStruct((rows_out, D_MODEL), F32),
        scratch_shapes=scratch,
        compiler_params=_cparams(("parallel",)),
        name=name,
    )(*args)


def _ffn_kernel(x_ref, xp_ref, xn_ref, nw_ref, sh_ref, sc_ref, g_ref, wv_ref, wg_ref, cwv_ref, cwg_ref,
                cbv_ref, cbg_ref, wd_ref, pw_ref, o_ref, h_scr, uv_scr, ug_scr, acc_scr, *, tm):
    i = pl.program_id(0)
    f = pl.program_id(1)

    @pl.when(f == 0)
    def _():
        def nm(xr):
            return (_rms(xr[...], nw_ref[...]) * (1.0 + sc_ref[...]) + sh_ref[...]).astype(BF16)
        h_scr[0:HALO, :] = nm(xp_ref)
        h_scr[HALO:HALO + tm, :] = nm(x_ref)
        h_scr[HALO + tm:HALO + tm + HALO, :] = nm(xn_ref)
        acc_scr[...] = jnp.zeros_like(acc_scr)

    r = i * tm + lax.broadcasted_iota(jnp.int32, (tm, 1), 0)
    has_left = jnp.where(_is_seq_start(r), 0.0, 1.0)
    has_right = jnp.where(_is_seq_start(r + 1), 0.0, 1.0)

    n_rh, n_sub = uv_scr.shape[0], uv_scr.shape[1]
    hr = tm // n_rh

    def conv(u_scr, cw_ref, cb_ref, cl, rows):
        left = u_scr[HALO - 1:HALO - 1 + hr, :] * has_left[rows]
        right = u_scr[HALO + 1:HALO + 1 + hr, :] * has_right[rows]
        return (left * cw_ref[0:1, cl] + u_scr[HALO:HALO + hr, :] * cw_ref[1:2, cl]
                + right * cw_ref[2:3, cl] + cb_ref[:, cl])

    for rh in range(n_rh):
        hh = h_scr[rh * hr:rh * hr + hr + 2 * HALO, :]
        for cb in range(n_sub):
            cl = slice(cb * FFN_SUB, (cb + 1) * FFN_SUB)
            uv_scr[rh, cb] = _dot(hh, wv_ref[:, cl])
            ug_scr[rh, cb] = _dot(hh, wg_ref[:, cl])
    for rh in range(n_rh):
        rows = slice(rh * hr, (rh + 1) * hr)
        acts = []
        for cb in range(n_sub):
            cl = slice(cb * FFN_SUB, (cb + 1) * FFN_SUB)
            val = conv(uv_scr.at[rh, cb], cwv_ref, cbv_ref, cl, rows)
            gate = conv(ug_scr.at[rh, cb], cwg_ref, cbg_ref, cl, rows)
            acts.append((val * (gate * _sigmoid(gate))).astype(BF16))
        acc_scr[rows, :] += _dot(jnp.concatenate(acts, axis=1), wd_ref[...])

    @pl.when(f == pl.num_programs(1) - 1)
    def _():
        o_ref[...] = x_ref[...] + g_ref[...] * _rms(acc_scr[...], pw_ref[...])


def _conv_ffn(x, nw, sh, sc, g, w_up, conv_w, conv_b, w_down, pw, *, tm, tf, rows_out, name):
    rows = x.shape[0]
    nt = rows_out // tm
    nf = D_FF // tf
    mi = _mod_index(tm)
    hb = tm // HALO
    last_halo = rows // HALO - 1
    vec_spec = pl.BlockSpec((1, D_MODEL), lambda i, f: (0, 0))
    mod_spec = pl.BlockSpec((None, 1, D_MODEL), lambda i, f: (mi(i), 0, 0))
    in_specs = [
        pl.BlockSpec((tm, D_MODEL), lambda i, f: (i, 0)),
        pl.BlockSpec((HALO, D_MODEL), lambda i, f: (jnp.maximum(i * hb - 1, 0), 0)),
        pl.BlockSpec((HALO, D_MODEL), lambda i, f: (jnp.minimum((i + 1) * hb, last_halo), 0)),
        vec_spec, mod_spec, mod_spec, mod_spec,
        pl.BlockSpec((D_MODEL, tf), lambda i, f: (0, f)),
        pl.BlockSpec((D_MODEL, tf), lambda i, f: (0, f + nf)),
        pl.BlockSpec((3, tf), lambda i, f: (0, f)),
        pl.BlockSpec((3, tf), lambda i, f: (0, f + nf)),
        pl.BlockSpec((1, tf), lambda i, f: (0, f)),
        pl.BlockSpec((1, tf), lambda i, f: (0, f + nf)),
        pl.BlockSpec((tf, D_MODEL), lambda i, f: (f, 0)),
        vec_spec,
    ]
    return pl.pallas_call(
        functools.partial(_ffn_kernel, tm=tm),
        grid=(nt, nf),
        in_specs=in_specs,
        out_specs=pl.BlockSpec((tm, D_MODEL), lambda i, f: (i, 0)),
        out_shape=jax.ShapeDtypeStruct((rows_out, D_MODEL), F32),
        scratch_shapes=[pltpu.VMEM((tm + 2 * HALO, D_MODEL), BF16),
                        pltpu.VMEM((FFN_ROW_BLOCKS, tf // FFN_SUB, tm // FFN_ROW_BLOCKS + 2 * HALO, FFN_SUB), F32),
                        pltpu.VMEM((FFN_ROW_BLOCKS, tf // FFN_SUB, tm // FFN_ROW_BLOCKS + 2 * HALO, FFN_SUB), F32),
                        pltpu.VMEM((tm, D_MODEL), F32)],
        compiler_params=_cparams(("parallel", "arbitrary")),
        name=name,
    )(x, x, x, nw, sh, sc, g, w_up, w_up, conv_w, conv_w, conv_b, conv_b, w_down, pw)


def _with_ones(v):
    return jnp.concatenate([v, jnp.ones_like(v)], axis=1)


def _gqa_attn_kernel(q_ref, kl_ref, vl_ref, kc_ref, vc_ref, o_ref, vla_scr, vca_scr):
    t = pl.program_id(2)
    n_lat_tiles = pl.num_programs(2) - 1
    group = GQA_HEADS // GQA_KV_HEADS
    kc = kc_ref[...]

    @pl.when(t == 0)
    def _():
        vla_scr[...] = _with_ones(vl_ref[...])
        vca_scr[...] = _with_ones(vc_ref[...])

    @pl.when(t < n_lat_tiles)
    def _():
        kl = kl_ref[...]

        def scores(gi):
            q = q_ref[:, gi * HEAD_DIM:(gi + 1) * HEAD_DIM]
            return _dot_nt(q, kc), _dot_nt(q, kl)

        s_c, s_l = scores(0)
        for gi in range(group):
            sl = slice(gi * HEAD_DIM, (gi + 1) * HEAD_DIM)
            s_next = scores(gi + 1) if gi + 1 < group else None
            m = jnp.maximum(jnp.max(s_c, axis=1, keepdims=True), jnp.max(s_l, axis=1, keepdims=True))
            p_c = jnp.exp(s_c - m).astype(BF16)
            p_l = jnp.exp(s_l - m).astype(BF16)
            hq = p_l.shape[0] // 2
            for rs in (slice(0, hq), slice(hq, 2 * hq)):
                o = _dot(p_c[rs], vca_scr[...]) + _dot(p_l[rs], vla_scr[...])
                o_ref[rs, sl] = (o[:, :HEAD_DIM] / o[:, HEAD_DIM:HEAD_DIM + 1]).astype(o_ref.dtype)
            if s_next is not None:
                s_c, s_l = s_next

    @pl.when(t == n_lat_tiles)
    def _():
        for gi in range(group):
            sl = slice(gi * HEAD_DIM, (gi + 1) * HEAD_DIM)
            q = q_ref[:, sl]
            s_c = _dot_nt(q, kc)
            p_c = jnp.exp(s_c - jnp.max(s_c, axis=1, keepdims=True)).astype(BF16)
            o = _dot(p_c, vca_scr[...])
            o_ref[:, sl] = (o[:, :HEAD_DIM] / o[:, HEAD_DIM:HEAD_DIM + 1]).astype(o_ref.dtype)


def _gqa_attention(z):
    tq = SEG
    n_lat = SEQ // tq
    group_w = (GQA_HEADS // GQA_KV_HEADS) * HEAD_DIM
    k_col = GQA_HEADS
    v_col = GQA_HEADS + GQA_KV_HEADS
    ctx_blk = R_LAT // SEG

    def q_rows(b, kh, t):
        return jnp.where(t < n_lat, b * n_lat + t, ctx_blk + b)

    return pl.pallas_call(
        _gqa_attn_kernel,
        grid=(BATCH, GQA_KV_HEADS, n_lat + 1),
        in_specs=[
            pl.BlockSpec((tq, group_w), lambda b, kh, t: (q_rows(b, kh, t), kh)),
            pl.BlockSpec((SEQ, HEAD_DIM), lambda b, kh, t: (b, k_col + kh)),
            pl.BlockSpec((SEQ, HEAD_DIM), lambda b, kh, t: (b, v_col + kh)),
            pl.BlockSpec((CTX_LEN, HEAD_DIM), lambda b, kh, t: (ctx_blk + b, k_col + kh)),
            pl.BlockSpec((CTX_LEN, HEAD_DIM), lambda b, kh, t: (ctx_blk + b, v_col + kh)),
        ],
        out_specs=pl.BlockSpec((tq, group_w), lambda b, kh, t: (q_rows(b, kh, t), kh)),
        out_shape=jax.ShapeDtypeStruct((R_ALL, GQA_HEADS * HEAD_DIM), BF16),
        scratch_shapes=[pltpu.VMEM((SEQ, 2 * HEAD_DIM), BF16),
                        pltpu.VMEM((CTX_LEN, 2 * HEAD_DIM), BF16)],
        compiler_params=_cparams(("parallel", "parallel", "arbitrary")),
        name="gqa_attention",
    )(z, z, z, z, z)


def _split3(x):
    hi = x.astype(BF16)
    r1 = x - hi.astype(F32)
    mid = r1.astype(BF16)
    lo = (r1 - mid.astype(F32)).astype(BF16)
    return hi, mid, lo


def _log_sigmoid(x):
    return jnp.minimum(x, 0.0) - jnp.log(1.0 + jnp.exp(-jnp.abs(x)))


MLSTM_AUG = MLSTM_DV + 128


def _mlstm_kernel(q_ref, k_ref, v_ref, g_ref, o_ref, c_scr, m_scr):
    d = pl.program_id(1)
    c = pl.program_id(2)
    nh = MLSTM_HEADS
    L = SEG
    scale = MLSTM_DQK ** -0.5

    @pl.when(c == 0)
    def _():
        c_scr[...] = jnp.zeros_like(c_scr)
        m_scr[...] = jnp.full_like(m_scr, M_INIT)

    ri = lax.broadcasted_iota(jnp.int32, (L, L), 0)
    ci = lax.broadcasted_iota(jnp.int32, (L, L), 1)
    diff = jnp.where(d == 0, ci - ri, ri - ci)
    sees = diff <= 0
    cum_t = jnp.where(diff >= 0, 1.0, 0.0).astype(BF16)

    g = g_ref[...]
    ig = g[0:nh]
    lf = _log_sigmoid(g[nh:2 * nh])
    b = sum(_dot(p, cum_t) for p in _split3(lf))
    g_tot = jnp.sum(lf, axis=1, keepdims=True)
    m_in = m_scr[:, 0:1]
    cr = ig - b
    lane = lax.broadcasted_iota(jnp.int32, (nh, L), 1)
    pm = cr
    for step in range(L.bit_length() - 1):
        sh = 1 << step
        prev = jnp.where(lane >= sh, pltpu.roll(pm, sh, 1), NEG)
        nxt = jnp.where(lane < L - sh, pltpu.roll(pm, L - sh, 1), NEG)
        pm = jnp.maximum(pm, jnp.where(d == 0, prev, nxt))
    big_m = jnp.maximum(m_in, pm)
    w_inter = jnp.exp(m_in - big_m) * scale
    floor = jnp.exp(-(b + big_m))
    a = g_tot - b + ig
    m_loc = jnp.max(a, axis=1, keepdims=True)
    w = jnp.exp(a - m_loc)
    m_new = jnp.maximum(g_tot + m_in, m_loc)
    a_old = jnp.exp(g_tot + m_in - m_new)
    a_new = jnp.exp(m_loc - m_new)
    slab = jnp.concatenate([big_m, w_inter, floor, jnp.zeros((128 - 3 * nh, L), F32)], axis=0)
    cols = slab.T

    ones = jnp.ones((L, MLSTM_AUG - MLSTM_DV), BF16)
    for h in range(nh):
        q = q_ref[:, h * MLSTM_DQK:(h + 1) * MLSTM_DQK]
        k = k_ref[:, h * MLSTM_DQK:(h + 1) * MLSTM_DQK]
        v_aug = jnp.concatenate([v_ref[:, h * MLSTM_DV:(h + 1) * MLSTM_DV], ones], axis=1)
        e = jnp.where(sees, cr[h:h + 1, :] - cols[:, h:h + 1], NEG)
        s = _dot_nt(q, k) * (jnp.exp(e) * scale)
        c_in = c_scr[h]
        num = _dot(s.astype(BF16), v_aug) + cols[:, nh + h:nh + h + 1] * _dot(q, c_in.astype(BF16))
        den = jnp.maximum(jnp.abs(num[:, MLSTM_DV:MLSTM_DV + 1]), cols[:, 2 * nh + h:2 * nh + h + 1])
        o_ref[:, h * MLSTM_DV:(h + 1) * MLSTM_DV] = num[:, :MLSTM_DV] / den

        kw_t = (k.astype(F32).T * w[h:h + 1, :]).astype(BF16)
        c_scr[h] = a_old[h:h + 1, :] * c_in + a_new[h:h + 1, :] * _dot(kw_t, v_aug)
    m_scr[...] = jnp.broadcast_to(m_new, m_scr.shape)


def _mlstm(z, gates_t):
    n_lat = SEQ // SEG
    ctx_blk = R_LAT // SEG
    nqk = MLSTM_HEADS * MLSTM_DQK
    nv = MLSTM_HEADS * MLSTM_DV

    def rb(b, d, c):
        lat = jnp.where(d == 0, c - 1, n_lat - c)
        return jnp.where(c == 0, ctx_blk + b, b * n_lat + lat)

    return pl.pallas_call(
        _mlstm_kernel,
        grid=(BATCH, 2, n_lat + 1),
        in_specs=[
            pl.BlockSpec((SEG, nqk), lambda b, d, c: (rb(b, d, c), 0)),
            pl.BlockSpec((SEG, nqk), lambda b, d, c: (rb(b, d, c), 1)),
            pl.BlockSpec((SEG, nv), lambda b, d, c: (rb(b, d, c), 2 * nqk // nv)),
            pl.BlockSpec((2 * MLSTM_HEADS, SEG), lambda b, d, c: (d, rb(b, d, c))),
        ],
        out_specs=pl.BlockSpec((None, SEG, nv), lambda b, d, c: (d, rb(b, d, c), 0)),
        out_shape=jax.ShapeDtypeStruct((2, R_ALL, nv), F32),
        scratch_shapes=[pltpu.VMEM((MLSTM_HEADS, MLSTM_DQK, MLSTM_AUG), F32),
                        pltpu.VMEM((MLSTM_HEADS, 128), F32)],
        compiler_params=_cparams(("parallel", "parallel", "arbitrary")),
        name="mlstm",
    )(z, z, z, gates_t)


DW_ROWS = 32
DW_COLS = 512


def _dwconv_ln_kernel(u_ref, up_ref, un_ref, w_ref, b_ref, lw_ref, lb_ref, o_ref, pad_scr, conv_scr, sh_scr):
    i = pl.program_id(0)
    tm = u_ref.shape[0]
    r0 = i * tm
    keep_prev = jnp.logical_not(_is_seq_start(r0))
    keep_next = jnp.logical_not(_is_seq_start(r0 + tm))
    pad_scr[0:HALO, :] = jnp.where(keep_prev, up_ref[...].astype(F32), 0.0)
    pad_scr[HALO:HALO + tm, :] = u_ref[...].astype(F32)
    pad_scr[HALO + tm:HALO + tm + HALO, :] = jnp.where(keep_next, un_ref[...].astype(F32), 0.0)

    rc = DW_ROWS
    cw = DW_COLS
    half = CONV_WIDTH // 2
    n_sh = sh_scr.shape[1]

    def chunk(cs, carry):
        cl = pl.ds(pl.multiple_of(cs * cw, cw), cw)
        for s in range(1, 8):
            sh_scr[s] = pad_scr[s:s + n_sh, cl]
        for ib in range(tm // rc):
            base = ib * rc
            acc = jnp.zeros((rc, cw), F32) + b_ref[:, cl]
            for kk in range(CONV_WIDTH):
                al, s = divmod(HALO - half + kk, 8)
                off = base + 8 * al
                tap = pad_scr[off:off + rc, cl] if s == 0 else sh_scr[s, off:off + rc, :]
                acc = acc + tap * w_ref[kk:kk + 1, cl]
            conv_scr[base:base + rc, cl] = acc
        return carry

    lax.fori_loop(0, D_MODEL // cw, chunk, 0)
    y = conv_scr[...]
    mu = jnp.mean(y, axis=-1, keepdims=True)
    yc = y - mu
    var = jnp.mean(yc * yc, axis=-1, keepdims=True)
    yn = yc * lax.rsqrt(var + EPS) * lw_ref[...] + lb_ref[...]
    o_ref[...] = (yn * _sigmoid(yn)).astype(o_ref.dtype)


def _dwconv_ln_swish(u, w_dw, b_dw, ln_w, ln_b):
    rows = u.shape[0]
    tm = SEG
    hb = tm // HALO
    last_halo = rows // HALO - 1
    vec_spec = pl.BlockSpec((1, D_MODEL), lambda i: (0, 0))
    return pl.pallas_call(
        _dwconv_ln_kernel,
        grid=(rows // tm,),
        in_specs=[
            pl.BlockSpec((tm, D_MODEL), lambda i: (i, 0)),
            pl.BlockSpec((HALO, D_MODEL), lambda i: (jnp.maximum(i * hb - 1, 0), 0)),
            pl.BlockSpec((HALO, D_MODEL), lambda i: (jnp.minimum((i + 1) * hb, last_halo), 0)),
            pl.BlockSpec((CONV_WIDTH, D_MODEL), lambda i: (0, 0)),
            vec_spec, vec_spec, vec_spec,
        ],
        out_specs=pl.BlockSpec((tm, D_MODEL), lambda i: (i, 0)),
        out_shape=jax.ShapeDtypeStruct((rows, D_MODEL), BF16),
        scratch_shapes=[pltpu.VMEM((tm + 2 * HALO, D_MODEL), F32),
                        pltpu.VMEM((tm, D_MODEL), F32),
                        pltpu.VMEM((8, tm + 2 * HALO - 8, DW_COLS), F32)],
        compiler_params=_cparams(("parallel",)),
        name="dwconv_ln_swish",
    )(u, u, u, w_dw, b_dw, ln_w, ln_b)


NAT_Q_ROWS = 4
NAT_BAND_ROWS = 12
NAT_TQ = NAT_Q_ROWS * GRID_W
NAT_BAND = NAT_BAND_ROWS * GRID_W


NAT_HEADS_PER_STEP = 4


def _nat_kernel(q_ref, kl_ref, vl_ref, kc_ref, vc_ref, bias_ref, o_ref, vla_scr, vca_scr):
    gq = pl.program_id(2)
    max_start = (SEQ - NAT_BAND) // NAT_TQ
    start = pl.multiple_of(jnp.clip(gq - 1, 0, max_start) * NAT_TQ, NAT_TQ)
    hd = NA_HEAD_DIM

    @pl.when(gq == 0)
    def _():
        for hh in range(NAT_HEADS_PER_STEP):
            sl = slice(hh * hd, (hh + 1) * hd)
            vla_scr[:, 2 * hh * hd:2 * (hh + 1) * hd] = _with_ones(vl_ref[:, sl])
            vca_scr[:, 2 * hh * hd:2 * (hh + 1) * hd] = _with_ones(vc_ref[:, sl])

    def scores(hh):
        sl = slice(hh * hd, (hh + 1) * hd)
        q = q_ref[:, sl]
        return _dot_nt(q, kl_ref[pl.ds(start, NAT_BAND), sl]) + bias_ref[hh], _dot_nt(q, kc_ref[:, sl])

    s_next = scores(0)
    for hh in range(NAT_HEADS_PER_STEP):
        sl = slice(hh * hd, (hh + 1) * hd)
        sl2 = slice(2 * hh * hd, 2 * (hh + 1) * hd)
        s_n, s_c = s_next
        if hh + 1 < NAT_HEADS_PER_STEP:
            s_next = scores(hh + 1)
        m = jnp.maximum(jnp.max(s_n, axis=1, keepdims=True), jnp.max(s_c, axis=1, keepdims=True))
        p_n = jnp.exp(s_n - m).astype(BF16)
        p_c = jnp.exp(s_c - m).astype(BF16)
        o = _dot(p_n, vla_scr[pl.ds(start, NAT_BAND), sl2]) + _dot(p_c, vca_scr[:, sl2])
        o_ref[:, sl] = (o[:, :hd] / o[:, hd:hd + 1]).astype(o_ref.dtype)


def _nat_bias_blocks():
    rows = SEQ // GRID_W
    n_groups = rows // NAT_Q_ROWS
    q_col = np.arange(GRID_W)
    c_start = np.clip(q_col - NA_COLS // 2, 0, GRID_W - NA_COLS)
    dc = q_col[None, :] - c_start[:, None]
    col_ok = (dc >= 0) & (dc < NA_COLS)
    masked = 2 * NA_ROWS - 1
    block = np.full((3, NAT_Q_ROWS, NAT_BAND_ROWS), masked, np.int64)
    for var, gq in enumerate((0, 1, n_groups - 1)):
        lo = int(np.clip(gq - 1, 0, (SEQ - NAT_BAND) // NAT_TQ)) * NAT_Q_ROWS
        for qr in range(NAT_Q_ROWS):
            q_row = gq * NAT_Q_ROWS + qr
            r_start = int(np.clip(q_row - NA_ROWS // 2, 0, rows - NA_ROWS))
            for kr in range(NAT_BAND_ROWS):
                k_row = lo + kr
                if 0 <= k_row - r_start < NA_ROWS:
                    block[var, qr, kr] = k_row - q_row + (NA_ROWS - 1)
    return col_ok, block


def _nat_bias(rpb):
    n_r, n_c = 2 * NA_ROWS - 1, 2 * NA_COLS - 1
    w = GRID_W
    col_ok, block = _nat_bias_blocks()
    lead = w - NA_COLS
    vext = jnp.pad(rpb, ((0, 0), (0, 0), (lead, 2 * w - lead - n_c)))
    skew = jnp.broadcast_to(vext[:, :, None, :], (NA_HEADS, n_r, w, 2 * w)).reshape(NA_HEADS, n_r, 2 * w * w)
    skew = skew[:, :, :w * (2 * w - 1)].reshape(NA_HEADS, n_r, w, 2 * w - 1)
    tb = jnp.where(col_ok[None, None], skew[..., w - 1:], NEG)
    tb = jnp.concatenate([tb, jnp.full((NA_HEADS, 1, w, w), NEG, F32)], axis=1)
    blocks = jnp.stack([tb[:, int(d)] for d in block.reshape(-1)], axis=1)
    blocks = blocks.reshape(NA_HEADS, 3, NAT_Q_ROWS, NAT_BAND_ROWS, w, w)
    return blocks.transpose(0, 1, 2, 4, 3, 5).reshape(NA_HEADS, 3, NAT_TQ, NAT_BAND)


def _nat_attention(z, rpb):
    n_groups = SEQ // NAT_TQ
    ctx_blk = R_LAT // SEG
    bias = _nat_bias(rpb)
    hps = NAT_HEADS_PER_STEP
    gw = hps * NA_HEAD_DIM

    def variant(gq):
        return (gq > 0).astype(jnp.int32) + (gq == n_groups - 1).astype(jnp.int32)

    k_col = NA_HEADS // hps
    v_col = 2 * NA_HEADS // hps
    return pl.pallas_call(
        _nat_kernel,
        grid=(BATCH, NA_HEADS // hps, n_groups),
        in_specs=[
            pl.BlockSpec((NAT_TQ, gw), lambda b, h, gq: (b * n_groups + gq, h)),
            pl.BlockSpec((SEQ, gw), lambda b, h, gq: (b, k_col + h)),
            pl.BlockSpec((SEQ, gw), lambda b, h, gq: (b, v_col + h)),
            pl.BlockSpec((CTX_LEN, gw), lambda b, h, gq: (ctx_blk + b, k_col + h)),
            pl.BlockSpec((CTX_LEN, gw), lambda b, h, gq: (ctx_blk + b, v_col + h)),
            pl.BlockSpec((hps, None, NAT_TQ, NAT_BAND), lambda b, h, gq: (h, variant(gq), 0, 0)),
        ],
        out_specs=pl.BlockSpec((NAT_TQ, gw), lambda b, h, gq: (b * n_groups + gq, h)),
        out_shape=jax.ShapeDtypeStruct((R_LAT, NA_HEADS * NA_HEAD_DIM), BF16),
        scratch_shapes=[pltpu.VMEM((SEQ, 2 * gw), BF16),
                        pltpu.VMEM((CTX_LEN, 2 * gw), BF16)],
        compiler_params=_cparams(("parallel", "parallel", "arbitrary")),
        name="nat_attention",
    )(z, z, z, z, z, bias)


def _rope_tables():
    pos = jnp.arange(SEQ)
    quarter = HEAD_DIM // 4
    freqs = ROPE_THETA ** (-jnp.arange(quarter, dtype=F32) / quarter)

    def cs(p):
        ang = p.astype(F32)[:, None] * freqs[None, :]
        return jnp.cos(ang), jnp.sin(ang)

    cr, sr = cs(pos // GRID_W)
    cc, sc = cs(pos % GRID_W)
    cos = jnp.concatenate([cr, cr, cc, cc], axis=1)
    sins = jnp.concatenate([-sr, sr, -sc, sc], axis=1)
    n_ctx = BATCH * CTX_LEN
    cos = jnp.concatenate([cos] * BATCH + [jnp.ones((n_ctx, HEAD_DIM), F32)], axis=0)
    sins = jnp.concatenate([sins] * BATCH + [jnp.zeros((n_ctx, HEAD_DIM), F32)], axis=0)
    return cos, sins


def kernel(x, c, ctx, c_ctx, mod_w, mod_b, norm_pre_mix, norm_post_mix, norm_pre_ffn, norm_post_ffn, ffn_w_up, ffn_conv_w, ffn_conv_b, ffn_w_down, gqa_w_in, gqa_q_norm, gqa_k_norm, gqa_w_out, mlstm_w_in, mlstm_b_gate, mlstm_out_norm, mlstm_w_out, conv_w_pw1, conv_b_pw1, conv_w_dw, conv_b_dw, conv_ln_w, conv_ln_b, conv_w_pw2, conv_b_pw2, nat_w_in, nat_rpb, nat_w_out):
    tm = 512
    xs = jnp.concatenate([x.reshape(R_LAT, D_MODEL), ctx.reshape(BATCH * CTX_LEN, D_MODEL)], axis=0)
    a8 = jnp.concatenate([c, c_ctx[None, :], jnp.zeros((8 - BATCH - 1, D_MODEL), F32)], axis=0)
    mods = _modulation(a8, mod_w, mod_b)
    mods = mods[:, :BATCH + 1].reshape(DEPTH, BATCH + 1, MOD_CHUNKS, 1, D_MODEL)
    cos, sins = _rope_tables()

    def row(v):
        return v.reshape(1, -1)

    for i in range(DEPTH):
        kind = i % 4
        last = i == DEPTH - 1
        rows_out = R_LAT if last else R_ALL
        sh1, sc1, g1, sh2, sc2, g2 = [mods[i, :, m] for m in range(MOD_CHUNKS)]
        pre = row(norm_pre_mix[i])
        post = row(norm_post_mix[i])
        if kind == 0:
            z = _project(xs, pre, sh1, sc1, gqa_w_in[0].astype(BF16), mode="gqa", tm=tm, tn=512,
                         extra=(row(gqa_q_norm[0]), row(gqa_k_norm[0]), cos, sins), name="gqa_in")
            a = _gqa_attention(z)
            xs = _out_project(a, gqa_w_out[0].astype(BF16), None, xs, post, g1, tm=tm, rows_out=rows_out,
                              name="gqa_out")
        elif kind == 1:
            n_main = 2 * MLSTM_HEADS * MLSTM_DQK + 2 * MLSTM_HEADS * MLSTM_DV
            n_gate = 4 * MLSTM_HEADS
            w_in = mlstm_w_in[0]
            w_g = jnp.pad(w_in[:, n_main:], ((0, 0), (0, 128 - n_gate))).astype(BF16)
            b_g = jnp.pad(mlstm_b_gate[0], (0, 128 - n_gate)).reshape(1, 128)
            z, gates_t = _project(xs, pre, sh1, sc1, w_in[:, :n_main].astype(BF16), mode="mlstm", tm=tm, tn=2048,
                                  extra=(w_g, b_g), name="mlstm_in")
            hdir = _mlstm(z, gates_t)
            xs = _out_project(None, mlstm_w_out[0].astype(BF16), None, xs, post, g1, tm=SEG, rows_out=rows_out,
                              name="mlstm_out", mlstm=(hdir, z, row(mlstm_out_norm[0])))
        elif kind == 2:
            u = _project(xs, pre, sh1, sc1, conv_w_pw1[0].astype(BF16), mode="glu", tm=tm, tn=1024,
                         bias=row(conv_b_pw1[0]), name="conformer_pw1")
            a = _dwconv_ln_swish(u, conv_w_dw[0], row(conv_b_dw[0]), row(conv_ln_w[0]), row(conv_ln_b[0]))
            xs = _out_project(a, conv_w_pw2[0].astype(BF16), row(conv_b_pw2[0]), xs, post, g1, tm=tm,
                              rows_out=rows_out, name="conformer_pw2")
        else:
            nq = NA_HEADS * NA_HEAD_DIM
            z = _project(xs, pre, sh1, sc1, nat_w_in[0].astype(BF16), mode="plain", tm=tm, tn=2048,
                         n_scaled=nq // 2048, scale=NA_HEAD_DIM ** -0.5, name="nat_in")
            a = _nat_attention(z, nat_rpb[0])
            xs = _out_project(a, nat_w_out[0].astype(BF16), None, xs, post, g1, tm=tm, rows_out=rows_out,
                              name="nat_out")
        xs = _conv_ffn(xs, row(norm_pre_ffn[i]), sh2, sc2, g2, ffn_w_up[i].astype(BF16), ffn_conv_w[i],
                       row(ffn_conv_b[i]), ffn_w_down[i].astype(BF16), row(norm_post_ffn[i]),
                       tm=tm, tf=512, rows_out=rows_out, name="conv_ffn")
    return xs.reshape(BATCH, SEQ, D_MODEL)
```

```python
import functools

import numpy as np
import jax
import jax.numpy as jnp
from jax import lax
from jax.experimental import pallas as pl
from jax.experimental.pallas import tpu as pltpu

F32 = jnp.float32
BF16 = jnp.bfloat16

D_MODEL = 2048
BATCH = 2
SEQ = 4096
DEPTH = 4
GRID_W = 64
CTX_LEN = 256
EPS = 1e-6
MOD_CHUNKS = 6
D_FF = 5632
GQA_HEADS = 16
GQA_KV_HEADS = 4
HEAD_DIM = 128
ROPE_THETA = 10000.0
MLSTM_HEADS = 8
MLSTM_DQK = 128
MLSTM_DV = 256
M_INIT = -1e30
CONV_WIDTH = 31
NA_HEADS = 16
NA_HEAD_DIM = 128
NA_ROWS = 8
NA_COLS = 16

R_LAT = BATCH * SEQ
R_ALL = R_LAT + BATCH * CTX_LEN
NEG = -1e30

VMEM_LIMIT = 56 * 1024 * 1024
SEG = 256
HALO = 16
FFN_SUB = 256
FFN_ROW_BLOCKS = 2
FFN_TF = 512


def _cparams(sem, flags=None):
    return pltpu.CompilerParams(dimension_semantics=sem, vmem_limit_bytes=VMEM_LIMIT, flags=flags)


def _sigmoid(x):
    return 1.0 / (1.0 + jnp.exp(-x))


def _rms(x, w):
    return x * lax.rsqrt(jnp.mean(x * x, axis=-1, keepdims=True) + EPS) * w


def _dot(a, b):
    return jnp.dot(a, b, preferred_element_type=F32)


def _dot_nt(a, b):
    return lax.dot_general(a, b, (((1,), (1,)), ((), ())), preferred_element_type=F32)


def _dot_tn(a, b):
    return lax.dot_general(a, b, (((0,), (0,)), ((), ())), preferred_element_type=F32)


def _is_seq_start(r):
    return (r & jnp.where(r < R_LAT, SEQ - 1, CTX_LEN - 1)) == 0


def _mod_index(tm):
    per_batch = SEQ // tm
    return lambda i: jnp.minimum(i // per_batch, BATCH)


def _mod_kernel(a_ref, w_ref, b_ref, o_ref):
    a = a_ref[...]
    s = (a * _sigmoid(a)).astype(BF16)
    o_ref[...] = _dot(s, w_ref[...].astype(BF16)) + b_ref[...]


def _modulation(a8, mod_w, mod_b):
    tn = 1024
    n = MOD_CHUNKS * D_MODEL
    return pl.pallas_call(
        _mod_kernel,
        grid=(DEPTH, n // tn),
        in_specs=[
            pl.BlockSpec((8, D_MODEL), lambda l, j: (0, 0)),
            pl.BlockSpec((None, D_MODEL, tn), lambda l, j: (l, 0, j)),
            pl.BlockSpec((None, 1, tn), lambda l, j: (l, 0, j)),
        ],
        out_specs=pl.BlockSpec((None, 8, tn), lambda l, j: (l, 0, j)),
        out_shape=jax.ShapeDtypeStruct((DEPTH, 8, n), F32),
        compiler_params=_cparams(("parallel", "parallel")),
        name="modulation",
    )(a8, mod_w, mod_b.reshape(DEPTH, 1, n))


def _norm_mod_to_scratch(x_ref, nw_ref, sh_ref, sc_ref, h_scr):
    @pl.when(pl.program_id(1) == 0)
    def _():
        h = _rms(x_ref[...], nw_ref[...]) * (1.0 + sc_ref[...]) + sh_ref[...]
        h_scr[...] = h.astype(BF16)


def _proj_plain_kernel(x_ref, nw_ref, sh_ref, sc_ref, w_ref, b_ref, o_ref, h_scr, *, n_scaled, scale):
    _norm_mod_to_scratch(x_ref, nw_ref, sh_ref, sc_ref, h_scr)
    y = _dot(h_scr[...], w_ref[...]) + b_ref[...]
    if n_scaled:
        y = y * jnp.where(pl.program_id(1) < n_scaled, scale, 1.0).astype(F32)
    o_ref[...] = y.astype(o_ref.dtype)


def _proj_mlstm_kernel(x_ref, nw_ref, sh_ref, sc_ref, w_ref, wg_ref, bg_ref, o_ref, gt_ref, h_scr):
    _norm_mod_to_scratch(x_ref, nw_ref, sh_ref, sc_ref, h_scr)

    @pl.when(pl.program_id(1) == 0)
    def _():
        gt_ref[...] = (_dot(h_scr[...], wg_ref[...]) + bg_ref[...]).T

    o_ref[...] = _dot(h_scr[...], w_ref[...]).astype(o_ref.dtype)


def _proj_glu_kernel(x_ref, nw_ref, sh_ref, sc_ref, wa_ref, wg_ref, ba_ref, bg_ref, o_ref, h_scr):
    _norm_mod_to_scratch(x_ref, nw_ref, sh_ref, sc_ref, h_scr)
    h = h_scr[...]
    a = _dot(h, wa_ref[...]) + ba_ref[...]
    g = _dot(h, wg_ref[...]) + bg_ref[...]
    o_ref[...] = (a * _sigmoid(g)).astype(o_ref.dtype)


def _rope(y, cos, sins, first_half):
    partner = jnp.where(first_half, pltpu.roll(y, 96, 1), pltpu.roll(y, 32, 1))
    return y * cos + partner * sins


def _proj_gqa_kernel(x_ref, nw_ref, sh_ref, sc_ref, w_ref, qn_ref, kn_ref, cos_ref, sin_ref, o_ref, h_scr):
    _norm_mod_to_scratch(x_ref, nw_ref, sh_ref, sc_ref, h_scr)
    j = pl.program_id(1)
    y = _dot(h_scr[...], w_ref[...])
    tn = y.shape[1]
    n_q = GQA_HEADS * HEAD_DIM // tn
    n_k = GQA_KV_HEADS * HEAD_DIM // tn

    def normed(nw, scale):
        cos = cos_ref[...]
        sins = sin_ref[...]
        lane = lax.broadcasted_iota(jnp.int32, cos.shape, 1)
        first_half = (lane & 63) < 32
        for s in range(tn // HEAD_DIM):
            ys = y[:, s * HEAD_DIM:(s + 1) * HEAD_DIM]
            ys = _rms(ys, nw) * scale
            o_ref[:, s * HEAD_DIM:(s + 1) * HEAD_DIM] = _rope(ys, cos, sins, first_half).astype(o_ref.dtype)

    @pl.when(j < n_q)
    def _():
        normed(qn_ref[...], HEAD_DIM ** -0.5)

    @pl.when(jnp.logical_and(j >= n_q, j < n_q + n_k))
    def _():
        normed(kn_ref[...], 1.0)

    @pl.when(j >= n_q + n_k)
    def _():
        o_ref[...] = y.astype(o_ref.dtype)


def _project(x, nw, sh, sc, w, *, mode, tm, tn, out_dtype=BF16, bias=None, n_scaled=0, scale=1.0,
             extra=None, name):
    rows = x.shape[0]
    n = w.shape[1]
    nt = rows // tm
    mi = _mod_index(tm)
    row_spec = pl.BlockSpec((tm, D_MODEL), lambda i, j: (i, 0))
    vec_spec = pl.BlockSpec((1, D_MODEL), lambda i, j: (0, 0))
    mod_spec = pl.BlockSpec((None, 1, D_MODEL), lambda i, j: (mi(i), 0, 0))
    common = [row_spec, vec_spec, mod_spec, mod_spec]
    args = [x, nw, sh, sc]
    if mode == "plain":
        n_out = n
        if bias is None:
            bias = jnp.zeros((1, n), F32)
        in_specs = common + [pl.BlockSpec((D_MODEL, tn), lambda i, j: (0, j)),
                             pl.BlockSpec((1, tn), lambda i, j: (0, j))]
        args += [w, bias]
        body = functools.partial(_proj_plain_kernel, n_scaled=n_scaled, scale=scale)
    elif mode == "glu":
        n_out = n // 2
        off = n_out // tn
        in_specs = common + [pl.BlockSpec((D_MODEL, tn), lambda i, j: (0, j)),
                             pl.BlockSpec((D_MODEL, tn), lambda i, j: (0, j + off)),
                             pl.BlockSpec((1, tn), lambda i, j: (0, j)),
                             pl.BlockSpec((1, tn), lambda i, j: (0, j + off))]
        args += [w, w, bias, bias]
        body = _proj_glu_kernel
    elif mode == "mlstm":
        n_out = n
        w_g, b_g = extra
        ng = w_g.shape[1]
        in_specs = common + [pl.BlockSpec((D_MODEL, tn), lambda i, j: (0, j)),
                             pl.BlockSpec((D_MODEL, ng), lambda i, j: (0, 0)),
                             pl.BlockSpec((1, ng), lambda i, j: (0, 0))]
        args += [w, w_g, b_g]
        return pl.pallas_call(
            _proj_mlstm_kernel,
            grid=(nt, n_out // tn),
            in_specs=in_specs,
            out_specs=[pl.BlockSpec((tm, tn), lambda i, j: (i, j)),
                       pl.BlockSpec((ng, tm), lambda i, j: (0, i))],
            out_shape=[jax.ShapeDtypeStruct((rows, n_out), out_dtype),
                       jax.ShapeDtypeStruct((ng, rows), F32)],
            scratch_shapes=[pltpu.VMEM((tm, D_MODEL), BF16)],
            compiler_params=_cparams(("parallel", "arbitrary")),
            name=name,
        )(*args)
    else:
        n_out = n
        qn, kn, cos, sins = extra
        head_spec = pl.BlockSpec((1, HEAD_DIM), lambda i, j: (0, 0))
        tab_spec = pl.BlockSpec((tm, HEAD_DIM), lambda i, j: (i, 0))
        in_specs = common + [pl.BlockSpec((D_MODEL, tn), lambda i, j: (0, j)),
                             head_spec, head_spec, tab_spec, tab_spec]
        args += [w, qn, kn, cos, sins]
        body = _proj_gqa_kernel
    return pl.pallas_call(
        body,
        grid=(nt, n_out // tn),
        in_specs=in_specs,
        out_specs=pl.BlockSpec((tm, tn), lambda i, j: (i, j)),
        out_shape=jax.ShapeDtypeStruct((rows, n_out), out_dtype),
        scratch_shapes=[pltpu.VMEM((tm, D_MODEL), BF16)],
        compiler_params=_cparams(("parallel", "arbitrary")),
        name=name,
    )(*args)


def _outproj_kernel(a_ref, w_ref, b_ref, x_ref, pw_ref, g_ref, o_ref):
    y = _dot(a_ref[...], w_ref[...]) + b_ref[...]
    o_ref[...] = x_ref[...] + g_ref[...] * _rms(y, pw_ref[...])


def _outproj_mlstm_kernel(hf_ref, hb_ref, og_ref, on_ref, w_ref, b_ref, x_ref, pw_ref, g_ref, o_ref, a_scr):
    for h in range(MLSTM_HEADS):
        sl = slice(h * MLSTM_DV, (h + 1) * MLSTM_DV)
        hh = hf_ref[:, sl] + hb_ref[:, sl]
        hh = _rms(hh, on_ref[:, sl]) * _sigmoid(og_ref[:, sl].astype(F32))
        a_scr[:, sl] = hh.astype(BF16)
    y = _dot(a_scr[...], w_ref[...]) + b_ref[...]
    o_ref[...] = x_ref[...] + g_ref[...] * _rms(y, pw_ref[...])


def _out_project(a, w, bias, x, pw, g, *, tm, rows_out, name, mlstm=None):
    nt = rows_out // tm
    mi = _mod_index(tm)
    row_spec = pl.BlockSpec((tm, D_MODEL), lambda i: (i, 0))
    vec_spec = pl.BlockSpec((1, D_MODEL), lambda i: (0, 0))
    w_spec = pl.BlockSpec((D_MODEL, D_MODEL), lambda i: (0, 0))
    mod_spec = pl.BlockSpec((None, 1, D_MODEL), lambda i: (mi(i), 0, 0))
    if bias is None:
        bias = jnp.zeros((1, D_MODEL), F32)
    if mlstm is None:
        body = _outproj_kernel
        in_specs = [row_spec, w_spec, vec_spec, row_spec, vec_spec, mod_spec]
        args = [a, w, bias, x, pw, g]
        scratch = []
    else:
        hdir, z, out_norm = mlstm
        body = _outproj_mlstm_kernel
        o_gate_block = (2 * MLSTM_HEADS * MLSTM_DQK + MLSTM_HEADS * MLSTM_DV) // D_MODEL
        in_specs = [pl.BlockSpec((None, tm, D_MODEL), lambda i: (0, i, 0)),
                    pl.BlockSpec((None, tm, D_MODEL), lambda i: (1, i, 0)),
                    pl.BlockSpec((tm, D_MODEL), lambda i: (i, o_gate_block)),
                    vec_spec, w_spec, vec_spec, row_spec, vec_spec, mod_spec]
        args = [hdir, hdir, z, out_norm, w, bias, x, pw, g]
        scratch = [pltpu.VMEM((tm, D_MODEL), BF16)]
    return pl.pallas_call(
        body,
        grid=(nt,),
        in_specs=in_specs,
        out_specs=row_spec,
        out_shape=jax.ShapeDtypeStruct((rows_out, D_MODEL), F32),
        scratch_shapes=scratch,
        compiler_params=_cparams(("parallel",)),
        name=name,
    )(*args)


def _ffn_kernel(x_ref, xp_ref, xn_ref, nw_ref, sh_ref, sc_ref, g_ref, wu_ref, cwv_ref, cwg_ref,
                cbv_ref, cbg_ref, wd_ref, pw_ref, o_ref, h_scr, uv_scr, ug_scr, acc_scr, *, tm):
    i = pl.program_id(0)
    f = pl.program_id(1)
    tf = wd_ref.shape[0]

    @pl.when(f == 0)
    def _():
        def nm(xr):
            return (_rms(xr[...], nw_ref[...]) * (1.0 + sc_ref[...]) + sh_ref[...]).astype(BF16)
        h_scr[0:HALO, :] = nm(xp_ref)
        h_scr[HALO:HALO + tm, :] = nm(x_ref)
        h_scr[HALO + tm:HALO + tm + HALO, :] = nm(xn_ref)
        acc_scr[...] = jnp.zeros_like(acc_scr)

    r = i * tm + lax.broadcasted_iota(jnp.int32, (tm, 1), 0)
    has_left = jnp.where(_is_seq_start(r), 0.0, 1.0)
    has_right = jnp.where(_is_seq_start(r + 1), 0.0, 1.0)

    n_rh, n_sub = uv_scr.shape[0], uv_scr.shape[1]
    hr = tm // n_rh

    def conv(u_scr, cw_ref, cb_ref, cl, rows):
        left = u_scr[HALO - 1:HALO - 1 + hr, :] * has_left[rows]
        right = u_scr[HALO + 1:HALO + 1 + hr, :] * has_right[rows]
        return (left * cw_ref[0:1, cl] + u_scr[HALO:HALO + hr, :] * cw_ref[1:2, cl]
                + right * cw_ref[2:3, cl] + cb_ref[:, cl])

    for rh in range(n_rh):
        hh = h_scr[rh * hr:rh * hr + hr + 2 * HALO, :]
        for cb in range(n_sub):
            cl = slice(cb * FFN_SUB, (cb + 1) * FFN_SUB)
            uv_scr[rh, cb] = _dot(hh, wu_ref[:, cl])
            ug_scr[rh, cb] = _dot(hh, wu_ref[:, tf + cb * FFN_SUB:tf + (cb + 1) * FFN_SUB])
    for rh in range(n_rh):
        rows = slice(rh * hr, (rh + 1) * hr)
        acts = []
        for cb in range(n_sub):
            cl = slice(cb * FFN_SUB, (cb + 1) * FFN_SUB)
            val = conv(uv_scr.at[rh, cb], cwv_ref, cbv_ref, cl, rows)
            gate = conv(ug_scr.at[rh, cb], cwg_ref, cbg_ref, cl, rows)
            acts.append((val * (gate * _sigmoid(gate))).astype(BF16))
        acc_scr[rows, :] += _dot(jnp.concatenate(acts, axis=1), wd_ref[...])

    @pl.when(f == pl.num_programs(1) - 1)
    def _():
        o_ref[...] = x_ref[...] + g_ref[...] * _rms(acc_scr[...], pw_ref[...])


def _tile_ffn_up(w_up, tf):
    nf = D_FF // tf
    w = w_up.reshape(DEPTH, D_MODEL, 2, nf, tf).transpose(0, 3, 1, 2, 4)
    return w.reshape(DEPTH, nf, D_MODEL, 2 * tf).astype(BF16)


def _conv_ffn(x, nw, sh, sc, g, w_up, conv_w, conv_b, w_down, pw, *, layer, tm, tf, rows_out, name):
    rows = x.shape[0]
    nt = rows_out // tm
    nf = D_FF // tf
    mi = _mod_index(tm)
    hb = tm // HALO
    last_halo = rows // HALO - 1
    vec_spec = pl.BlockSpec((1, D_MODEL), lambda i, f: (0, 0))
    mod_spec = pl.BlockSpec((None, 1, D_MODEL), lambda i, f: (mi(i), 0, 0))
    in_specs = [
        pl.BlockSpec((tm, D_MODEL), lambda i, f: (i, 0)),
        pl.BlockSpec((HALO, D_MODEL), lambda i, f: (jnp.maximum(i * hb - 1, 0), 0)),
        pl.BlockSpec((HALO, D_MODEL), lambda i, f: (jnp.minimum((i + 1) * hb, last_halo), 0)),
        vec_spec, mod_spec, mod_spec, mod_spec,
        pl.BlockSpec((None, None, D_MODEL, 2 * tf), lambda i, f: (layer, f, 0, 0)),
        pl.BlockSpec((None, 3, tf), lambda i, f: (layer, 0, f)),
        pl.BlockSpec((None, 3, tf), lambda i, f: (layer, 0, f + nf)),
        pl.BlockSpec((None, 1, tf), lambda i, f: (layer, 0, f)),
        pl.BlockSpec((None, 1, tf), lambda i, f: (layer, 0, f + nf)),
        pl.BlockSpec((None, tf, D_MODEL), lambda i, f: (layer, f, 0)),
        vec_spec,
    ]
    return pl.pallas_call(
        functools.partial(_ffn_kernel, tm=tm),
        grid=(nt, nf),
        in_specs=in_specs,
        out_specs=pl.BlockSpec((tm, D_MODEL), lambda i, f: (i, 0)),
        out_shape=jax.ShapeDtypeStruct((rows_out, D_MODEL), F32),
        scratch_shapes=[pltpu.VMEM((tm + 2 * HALO, D_MODEL), BF16),
                        pltpu.VMEM((FFN_ROW_BLOCKS, tf // FFN_SUB, tm // FFN_ROW_BLOCKS + 2 * HALO, FFN_SUB), F32),
                        pltpu.VMEM((FFN_ROW_BLOCKS, tf // FFN_SUB, tm // FFN_ROW_BLOCKS + 2 * HALO, FFN_SUB), F32),
                        pltpu.VMEM((tm, D_MODEL), F32)],
        compiler_params=_cparams(("parallel", "arbitrary")),
        name=name,
    )(x, x, x, nw, sh, sc, g, w_up, conv_w, conv_w, conv_b, conv_b, w_down, pw)


def _with_ones(v):
    return jnp.concatenate([v, jnp.ones_like(v)], axis=1)


GQA_SCORES_AHEAD = 2


def _gqa_attn_kernel(q_ref, kl_ref, vl_ref, kc_ref, vc_ref, o_ref, vla_scr, vca_scr):
    t = pl.program_id(2)
    n_lat_tiles = pl.num_programs(2) - 1
    group = GQA_HEADS // GQA_KV_HEADS
    kc = kc_ref[...]

    @pl.when(t == 0)
    def _():
        vla_scr[...] = _with_ones(vl_ref[...])
        vca_scr[...] = _with_ones(vc_ref[...])

    @pl.when(t < n_lat_tiles)
    def _():
        kl = kl_ref[...]

        def scores(gi):
            q = q_ref[:, gi * HEAD_DIM:(gi + 1) * HEAD_DIM]
            return _dot_nt(q, kc), _dot_nt(q, kl)

        ahead = GQA_SCORES_AHEAD
        pending = [scores(gi) for gi in range(min(ahead, group))]
        for gi in range(group):
            sl = slice(gi * HEAD_DIM, (gi + 1) * HEAD_DIM)
            s_c, s_l = pending.pop(0)
            if gi + ahead < group:
                pending.append(scores(gi + ahead))
            m = jnp.maximum(jnp.max(s_c, axis=1, keepdims=True), jnp.max(s_l, axis=1, keepdims=True))
            p_c = jnp.exp(s_c - m).astype(BF16)
            p_l = jnp.exp(s_l - m).astype(BF16)
            hq = p_l.shape[0] // 2
            for rs in (slice(0, hq), slice(hq, 2 * hq)):
                o = _dot(p_c[rs], vca_scr[...]) + _dot(p_l[rs], vla_scr[...])
                o_ref[rs, sl] = (o[:, :HEAD_DIM] / o[:, HEAD_DIM:HEAD_DIM + 1]).astype(o_ref.dtype)

    @pl.when(t == n_lat_tiles)
    def _():
        for gi in range(group):
            sl = slice(gi * HEAD_DIM, (gi + 1) * HEAD_DIM)
            q = q_ref[:, sl]
            s_c = _dot_nt(q, kc)
            p_c = jnp.exp(s_c - jnp.max(s_c, axis=1, keepdims=True)).astype(BF16)
            o = _dot(p_c, vca_scr[...])
            o_ref[:, sl] = (o[:, :HEAD_DIM] / o[:, HEAD_DIM:HEAD_DIM + 1]).astype(o_ref.dtype)


def _gqa_attention(z):
    tq = SEG
    n_lat = SEQ // tq
    group_w = (GQA_HEADS // GQA_KV_HEADS) * HEAD_DIM
    k_col = GQA_HEADS
    v_col = GQA_HEADS + GQA_KV_HEADS
    ctx_blk = R_LAT // SEG

    def q_rows(b, kh, t):
        return jnp.where(t < n_lat, b * n_lat + t, ctx_blk + b)

    return pl.pallas_call(
        _gqa_attn_kernel,
        grid=(BATCH, GQA_KV_HEADS, n_lat + 1),
        in_specs=[
            pl.BlockSpec((tq, group_w), lambda b, kh, t: (q_rows(b, kh, t), kh)),
            pl.BlockSpec((SEQ, HEAD_DIM), lambda b, kh, t: (b, k_col + kh)),
            pl.BlockSpec((SEQ, HEAD_DIM), lambda b, kh, t: (b, v_col + kh)),
            pl.BlockSpec((CTX_LEN, HEAD_DIM), lambda b, kh, t: (ctx_blk + b, k_col + kh)),
            pl.BlockSpec((CTX_LEN, HEAD_DIM), lambda b, kh, t: (ctx_blk + b, v_col + kh)),
        ],
        out_specs=pl.BlockSpec((tq, group_w), lambda b, kh, t: (q_rows(b, kh, t), kh)),
        out_shape=jax.ShapeDtypeStruct((R_ALL, GQA_HEADS * HEAD_DIM), BF16),
        scratch_shapes=[pltpu.VMEM((SEQ, 2 * HEAD_DIM), BF16),
                        pltpu.VMEM((CTX_LEN, 2 * HEAD_DIM), BF16)],
        compiler_params=_cparams(("parallel", "parallel", "arbitrary")),
        name="gqa_attention",
    )(z, z, z, z, z)


def _split3(x):
    hi = x.astype(BF16)
    r1 = x - hi.astype(F32)
    mid = r1.astype(BF16)
    lo = (r1 - mid.astype(F32)).astype(BF16)
    return hi, mid, lo


def _log_sigmoid(x):
    return jnp.minimum(x, 0.0) - jnp.log(1.0 + jnp.exp(-jnp.abs(x)))


MLSTM_AUG = MLSTM_DV + 128


def _mlstm_kernel(q_ref, k_ref, v_ref, g_ref, o_ref, c_scr, m_scr):
    d = pl.program_id(1)
    c = pl.program_id(2)
    nh = MLSTM_HEADS
    L = SEG
    scale = MLSTM_DQK ** -0.5

    @pl.when(c == 0)
    def _():
        c_scr[...] = jnp.zeros_like(c_scr)
        m_scr[...] = jnp.full_like(m_scr, M_INIT)

    ri = lax.broadcasted_iota(jnp.int32, (L, L), 0)
    ci = lax.broadcasted_iota(jnp.int32, (L, L), 1)
    diff = jnp.where(d == 0, ci - ri, ri - ci)
    sees = diff <= 0
    cum_t = jnp.where(diff >= 0, 1.0, 0.0).astype(BF16)

    g = g_ref[...]
    ig = g[0:nh]
    lf = _log_sigmoid(g[nh:2 * nh])
    b = sum(_dot(p, cum_t) for p in _split3(lf))
    g_tot = jnp.sum(lf, axis=1, keepdims=True)
    m_in = m_scr[:, 0:1]
    cr = ig - b
    lane = lax.broadcasted_iota(jnp.int32, (nh, L), 1)
    pm = cr
    for step in range(L.bit_length() - 1):
        sh = 1 << step
        prev = jnp.where(lane >= sh, pltpu.roll(pm, sh, 1), NEG)
        nxt = jnp.where(lane < L - sh, pltpu.roll(pm, L - sh, 1), NEG)
        pm = jnp.maximum(pm, jnp.where(d == 0, prev, nxt))
    big_m = jnp.maximum(m_in, pm)
    w_inter = jnp.exp(m_in - big_m) * scale
    floor = jnp.exp(-(b + big_m))
    a = g_tot - b + ig
    m_loc = jnp.max(a, axis=1, keepdims=True)
    w = jnp.exp(a - m_loc)
    m_new = jnp.maximum(g_tot + m_in, m_loc)
    a_old = jnp.exp(g_tot + m_in - m_new)
    a_new = jnp.exp(m_loc - m_new)
    slab = jnp.concatenate([big_m, w_inter, floor, jnp.zeros((128 - 3 * nh, L), F32)], axis=0)
    cols = slab.T

    ones = jnp.ones((L, MLSTM_AUG - MLSTM_DV), BF16)
    for h in range(nh):
        q = q_ref[:, h * MLSTM_DQK:(h + 1) * MLSTM_DQK]
        k = k_ref[:, h * MLSTM_DQK:(h + 1) * MLSTM_DQK]
        v_aug = jnp.concatenate([v_ref[:, h * MLSTM_DV:(h + 1) * MLSTM_DV], ones], axis=1)
        e = jnp.where(sees, cr[h:h + 1, :] - cols[:, h:h + 1], NEG)
        s = _dot_nt(q, k) * (jnp.exp(e) * scale)
        c_in = c_scr[h]
        num = _dot(s.astype(BF16), v_aug) + cols[:, nh + h:nh + h + 1] * _dot(q, c_in.astype(BF16))
        den = jnp.maximum(jnp.abs(num[:, MLSTM_DV:MLSTM_DV + 1]), cols[:, 2 * nh + h:2 * nh + h + 1])
        o_ref[:, h * MLSTM_DV:(h + 1) * MLSTM_DV] = num[:, :MLSTM_DV] / den

        kw_t = (k.astype(F32).T * w[h:h + 1, :]).astype(BF16)
        c_scr[h] = a_old[h:h + 1, :] * c_in + a_new[h:h + 1, :] * _dot(kw_t, v_aug)
    m_scr[...] = jnp.broadcast_to(m_new, m_scr.shape)


def _mlstm(z, gates_t):
    n_lat = SEQ // SEG
    ctx_blk = R_LAT // SEG
    nqk = MLSTM_HEADS * MLSTM_DQK
    nv = MLSTM_HEADS * MLSTM_DV

    def rb(b, d, c):
        lat = jnp.where(d == 0, c - 1, n_lat - c)
        return jnp.where(c == 0, ctx_blk + b, b * n_lat + lat)

    return pl.pallas_call(
        _mlstm_kernel,
        grid=(BATCH, 2, n_lat + 1),
        in_specs=[
            pl.BlockSpec((SEG, nqk), lambda b, d, c: (rb(b, d, c), 0)),
            pl.BlockSpec((SEG, nqk), lambda b, d, c: (rb(b, d, c), 1)),
            pl.BlockSpec((SEG, nv), lambda b, d, c: (rb(b, d, c), 2 * nqk // nv)),
            pl.BlockSpec((2 * MLSTM_HEADS, SEG), lambda b, d, c: (d, rb(b, d, c))),
        ],
        out_specs=pl.BlockSpec((None, SEG, nv), lambda b, d, c: (d, rb(b, d, c), 0)),
        out_shape=jax.ShapeDtypeStruct((2, R_ALL, nv), F32),
        scratch_shapes=[pltpu.VMEM((MLSTM_HEADS, MLSTM_DQK, MLSTM_AUG), F32),
                        pltpu.VMEM((MLSTM_HEADS, 128), F32)],
        compiler_params=_cparams(("parallel", "parallel", "arbitrary")),
        name="mlstm",
    )(z, z, z, gates_t)


DW_ROWS = 32
DW_COLS = 512


def _dwconv_ln_kernel(u_ref, up_ref, un_ref, w_ref, b_ref, lw_ref, lb_ref, o_ref, pad_scr, conv_scr, sh_scr):
    i = pl.program_id(0)
    tm = u_ref.shape[0]
    r0 = i * tm
    keep_prev = jnp.logical_not(_is_seq_start(r0))
    keep_next = jnp.logical_not(_is_seq_start(r0 + tm))
    pad_scr[0:HALO, :] = jnp.where(keep_prev, up_ref[...].astype(F32), 0.0)
    pad_scr[HALO:HALO + tm, :] = u_ref[...].astype(F32)
    pad_scr[HALO + tm:HALO + tm + HALO, :] = jnp.where(keep_next, un_ref[...].astype(F32), 0.0)

    rc = DW_ROWS
    cw = DW_COLS
    half = CONV_WIDTH // 2
    n_sh = sh_scr.shape[1]

    def chunk(cs, carry):
        cl = pl.ds(pl.multiple_of(cs * cw, cw), cw)
        for s in range(1, 8):
            sh_scr[s] = pad_scr[s:s + n_sh, cl]
        for ib in range(tm // rc):
            base = ib * rc
            acc = jnp.zeros((rc, cw), F32) + b_ref[:, cl]
            for kk in range(CONV_WIDTH):
                al, s = divmod(HALO - half + kk, 8)
                off = base + 8 * al
                tap = pad_scr[off:off + rc, cl] if s == 0 else sh_scr[s, off:off + rc, :]
                acc = acc + tap * w_ref[kk:kk + 1, cl]
            conv_scr[base:base + rc, cl] = acc
        return carry

    lax.fori_loop(0, D_MODEL // cw, chunk, 0)
    y = conv_scr[...]
    mu = jnp.mean(y, axis=-1, keepdims=True)
    yc = y - mu
    var = jnp.mean(yc * yc, axis=-1, keepdims=True)
    yn = yc * lax.rsqrt(var + EPS) * lw_ref[...] + lb_ref[...]
    o_ref[...] = (yn * _sigmoid(yn)).astype(o_ref.dtype)


def _dwconv_ln_swish(u, w_dw, b_dw, ln_w, ln_b):
    rows = u.shape[0]
    tm = SEG
    hb = tm // HALO
    last_halo = rows // HALO - 1
    vec_spec = pl.BlockSpec((1, D_MODEL), lambda i: (0, 0))
    return pl.pallas_call(
        _dwconv_ln_kernel,
        grid=(rows // tm,),
        in_specs=[
            pl.BlockSpec((tm, D_MODEL), lambda i: (i, 0)),
            pl.BlockSpec((HALO, D_MODEL), lambda i: (jnp.maximum(i * hb - 1, 0), 0)),
            pl.BlockSpec((HALO, D_MODEL), lambda i: (jnp.minimum((i + 1) * hb, last_halo), 0)),
            pl.BlockSpec((CONV_WIDTH, D_MODEL), lambda i: (0, 0)),
            vec_spec, vec_spec, vec_spec,
        ],
        out_specs=pl.BlockSpec((tm, D_MODEL), lambda i: (i, 0)),
        out_shape=jax.ShapeDtypeStruct((rows, D_MODEL), BF16),
        scratch_shapes=[pltpu.VMEM((tm + 2 * HALO, D_MODEL), F32),
                        pltpu.VMEM((tm, D_MODEL), F32),
                        pltpu.VMEM((8, tm + 2 * HALO - 8, DW_COLS), F32)],
        compiler_params=_cparams(("parallel",)),
        name="dwconv_ln_swish",
    )(u, u, u, w_dw, b_dw, ln_w, ln_b)


NAT_Q_ROWS = 4
NAT_BAND_ROWS = 12
NAT_TQ = NAT_Q_ROWS * GRID_W
NAT_BAND = NAT_BAND_ROWS * GRID_W


NAT_HEADS_PER_STEP = 4


def _nat_kernel(q_ref, kl_ref, vl_ref, kc_ref, vc_ref, bias_ref, o_ref, vla_scr, vca_scr):
    gq = pl.program_id(2)
    max_start = (SEQ - NAT_BAND) // NAT_TQ
    start = pl.multiple_of(jnp.clip(gq - 1, 0, max_start) * NAT_TQ, NAT_TQ)
    hd = NA_HEAD_DIM

    @pl.when(gq == 0)
    def _():
        for hh in range(NAT_HEADS_PER_STEP):
            sl = slice(hh * hd, (hh + 1) * hd)
            vla_scr[:, 2 * hh * hd:2 * (hh + 1) * hd] = _with_ones(vl_ref[:, sl])
            vca_scr[:, 2 * hh * hd:2 * (hh + 1) * hd] = _with_ones(vc_ref[:, sl])

    def scores(hh):
        sl = slice(hh * hd, (hh + 1) * hd)
        q = q_ref[:, sl]
        return _dot_nt(q, kl_ref[pl.ds(start, NAT_BAND), sl]) + bias_ref[hh], _dot_nt(q, kc_ref[:, sl])

    s_next = scores(0)
    for hh in range(NAT_HEADS_PER_STEP):
        sl = slice(hh * hd, (hh + 1) * hd)
        sl2 = slice(2 * hh * hd, 2 * (hh + 1) * hd)
        s_n, s_c = s_next
        if hh + 1 < NAT_HEADS_PER_STEP:
            s_next = scores(hh + 1)
        m = jnp.maximum(jnp.max(s_n, axis=1, keepdims=True), jnp.max(s_c, axis=1, keepdims=True))
        p_n = jnp.exp(s_n - m).astype(BF16)
        p_c = jnp.exp(s_c - m).astype(BF16)
        o = _dot(p_n, vla_scr[pl.ds(start, NAT_BAND), sl2]) + _dot(p_c, vca_scr[:, sl2])
        o_ref[:, sl] = (o[:, :hd] / o[:, hd:hd + 1]).astype(o_ref.dtype)


def _nat_bias_blocks():
    rows = SEQ // GRID_W
    n_groups = rows // NAT_Q_ROWS
    q_col = np.arange(GRID_W)
    c_start = np.clip(q_col - NA_COLS // 2, 0, GRID_W - NA_COLS)
    dc = q_col[None, :] - c_start[:, None]
    col_ok = (dc >= 0) & (dc < NA_COLS)
    masked = 2 * NA_ROWS - 1
    block = np.full((3, NAT_Q_ROWS, NAT_BAND_ROWS), masked, np.int64)
    for var, gq in enumerate((0, 1, n_groups - 1)):
        lo = int(np.clip(gq - 1, 0, (SEQ - NAT_BAND) // NAT_TQ)) * NAT_Q_ROWS
        for qr in range(NAT_Q_ROWS):
            q_row = gq * NAT_Q_ROWS + qr
            r_start = int(np.clip(q_row - NA_ROWS // 2, 0, rows - NA_ROWS))
            for kr in range(NAT_BAND_ROWS):
                k_row = lo + kr
                if 0 <= k_row - r_start < NA_ROWS:
                    block[var, qr, kr] = k_row - q_row + (NA_ROWS - 1)
    return col_ok, block


def _nat_bias(rpb):
    n_r, n_c = 2 * NA_ROWS - 1, 2 * NA_COLS - 1
    w = GRID_W
    col_ok, block = _nat_bias_blocks()
    lead = w - NA_COLS
    vext = jnp.pad(rpb, ((0, 0), (0, 0), (lead, 2 * w - lead - n_c)))
    skew = jnp.broadcast_to(vext[:, :, None, :], (NA_HEADS, n_r, w, 2 * w)).reshape(NA_HEADS, n_r, 2 * w * w)
    skew = skew[:, :, :w * (2 * w - 1)].reshape(NA_HEADS, n_r, w, 2 * w - 1)
    tb = jnp.where(col_ok[None, None], skew[..., w - 1:], NEG)
    tb = jnp.concatenate([tb, jnp.full((NA_HEADS, 1, w, w), NEG, F32)], axis=1)
    strips = [jnp.concatenate([tb[:, int(d)] for d in block[var, qr]], axis=-1)
              for var in range(3) for qr in range(NAT_Q_ROWS)]
    return jnp.stack(strips, axis=1).reshape(NA_HEADS, 3, NAT_TQ, NAT_BAND)


def _nat_attention(z, rpb):
    n_groups = SEQ // NAT_TQ
    ctx_blk = R_LAT // SEG
    bias = _nat_bias(rpb)
    hps = NAT_HEADS_PER_STEP
    gw = hps * NA_HEAD_DIM

    def variant(gq):
        return (gq > 0).astype(jnp.int32) + (gq == n_groups - 1).astype(jnp.int32)

    k_col = NA_HEADS // hps
    v_col = 2 * NA_HEADS // hps
    return pl.pallas_call(
        _nat_kernel,
        grid=(BATCH, NA_HEADS // hps, n_groups),
        in_specs=[
            pl.BlockSpec((NAT_TQ, gw), lambda b, h, gq: (b * n_groups + gq, h)),
            pl.BlockSpec((SEQ, gw), lambda b, h, gq: (b, k_col + h)),
            pl.BlockSpec((SEQ, gw), lambda b, h, gq: (b, v_col + h)),
            pl.BlockSpec((CTX_LEN, gw), lambda b, h, gq: (ctx_blk + b, k_col + h)),
            pl.BlockSpec((CTX_LEN, gw), lambda b, h, gq: (ctx_blk + b, v_col + h)),
            pl.BlockSpec((hps, None, NAT_TQ, NAT_BAND), lambda b, h, gq: (h, variant(gq), 0, 0)),
        ],
        out_specs=pl.BlockSpec((NAT_TQ, gw), lambda b, h, gq: (b * n_groups + gq, h)),
        out_shape=jax.ShapeDtypeStruct((R_LAT, NA_HEADS * NA_HEAD_DIM), BF16),
        scratch_shapes=[pltpu.VMEM((SEQ, 2 * gw), BF16),
                        pltpu.VMEM((CTX_LEN, 2 * gw), BF16)],
        compiler_params=_cparams(("parallel", "parallel", "arbitrary")),
        name="nat_attention",
    )(z, z, z, z, z, bias)


def _rope_tables():
    pos = jnp.arange(SEQ)
    quarter = HEAD_DIM // 4
    freqs = ROPE_THETA ** (-jnp.arange(quarter, dtype=F32) / quarter)

    def cs(p):
        ang = p.astype(F32)[:, None] * freqs[None, :]
        return jnp.cos(ang), jnp.sin(ang)

    cr, sr = cs(pos // GRID_W)
    cc, sc = cs(pos % GRID_W)
    cos = jnp.concatenate([cr, cr, cc, cc], axis=1)
    sins = jnp.concatenate([-sr, sr, -sc, sc], axis=1)
    n_ctx = BATCH * CTX_LEN
    cos = jnp.concatenate([cos] * BATCH + [jnp.ones((n_ctx, HEAD_DIM), F32)], axis=0)
    sins = jnp.concatenate([sins] * BATCH + [jnp.zeros((n_ctx, HEAD_DIM), F32)], axis=0)
    return cos, sins


def kernel(x, c, ctx, c_ctx, mod_w, mod_b, norm_pre_mix, norm_post_mix, norm_pre_ffn, norm_post_ffn, ffn_w_up, ffn_conv_w, ffn_conv_b, ffn_w_down, gqa_w_in, gqa_q_norm, gqa_k_norm, gqa_w_out, mlstm_w_in, mlstm_b_gate, mlstm_out_norm, mlstm_w_out, conv_w_pw1, conv_b_pw1, conv_w_dw, conv_b_dw, conv_ln_w, conv_ln_b, conv_w_pw2, conv_b_pw2, nat_w_in, nat_rpb, nat_w_out):
    tm = 512
    xs = jnp.concatenate([x.reshape(R_LAT, D_MODEL), ctx.reshape(BATCH * CTX_LEN, D_MODEL)], axis=0)
    a8 = jnp.concatenate([c, c_ctx[None, :], jnp.zeros((8 - BATCH - 1, D_MODEL), F32)], axis=0)
    mods = _modulation(a8, mod_w, mod_b)
    mods = mods[:, :BATCH + 1].reshape(DEPTH, BATCH + 1, MOD_CHUNKS, 1, D_MODEL)
    cos, sins = _rope_tables()
    w_up_tiled = _tile_ffn_up(ffn_w_up, FFN_TF)
    w_down_bf16 = ffn_w_down.astype(BF16)
    conv_b3 = ffn_conv_b.reshape(DEPTH, 1, 2 * D_FF)

    def row(v):
        return v.reshape(1, -1)

    for i in range(DEPTH):
        kind = i % 4
        last = i == DEPTH - 1
        rows_out = R_LAT if last else R_ALL
        sh1, sc1, g1, sh2, sc2, g2 = [mods[i, :, m] for m in range(MOD_CHUNKS)]
        pre = row(norm_pre_mix[i])
        post = row(norm_post_mix[i])
        if kind == 0:
            z = _project(xs, pre, sh1, sc1, gqa_w_in[0].astype(BF16), mode="gqa", tm=tm, tn=512,
                         extra=(row(gqa_q_norm[0]), row(gqa_k_norm[0]), cos, sins), name="gqa_in")
            a = _gqa_attention(z)
            xs = _out_project(a, gqa_w_out[0].astype(BF16), None, xs, post, g1, tm=tm, rows_out=rows_out,
                              name="gqa_out")
        elif kind == 1:
            n_main = 2 * MLSTM_HEADS * MLSTM_DQK + 2 * MLSTM_HEADS * MLSTM_DV
            n_gate = 4 * MLSTM_HEADS
            w_in = mlstm_w_in[0]
            w_g = jnp.pad(w_in[:, n_main:], ((0, 0), (0, 128 - n_gate))).astype(BF16)
            b_g = jnp.pad(mlstm_b_gate[0], (0, 128 - n_gate)).reshape(1, 128)
            z, gates_t = _project(xs, pre, sh1, sc1, w_in[:, :n_main].astype(BF16), mode="mlstm", tm=tm, tn=2048,
                                  extra=(w_g, b_g), name="mlstm_in")
            hdir = _mlstm(z, gates_t)
            xs = _out_project(None, mlstm_w_out[0].astype(BF16), None, xs, post, g1, tm=SEG, rows_out=rows_out,
                              name="mlstm_out", mlstm=(hdir, z, row(mlstm_out_norm[0])))
        elif kind == 2:
            u = _project(xs, pre, sh1, sc1, conv_w_pw1[0].astype(BF16), mode="glu", tm=tm, tn=1024,
                         bias=row(conv_b_pw1[0]), name="conformer_pw1")
            a = _dwconv_ln_swish(u, conv_w_dw[0], row(conv_b_dw[0]), row(conv_ln_w[0]), row(conv_ln_b[0]))
            xs = _out_project(a, conv_w_pw2[0].astype(BF16), row(conv_b_pw2[0]), xs, post, g1, tm=tm,
                              rows_out=rows_out, name="conformer_pw2")
        else:
            nq = NA_HEADS * NA_HEAD_DIM
            z = _project(xs, pre, sh1, sc1, nat_w_in[0].astype(BF16), mode="plain", tm=tm, tn=2048,
                         n_scaled=nq // 2048, scale=NA_HEAD_DIM ** -0.5, name="nat_in")
            a = _nat_attention(z, nat_rpb[0])
            xs = _out_project(a, nat_w_out[0].astype(BF16), None, xs, post, g1, tm=tm, rows_out=rows_out,
                              name="nat_out")
        xs = _conv_ffn(xs, row(norm_pre_ffn[i]), sh2, sc2, g2, w_up_tiled, ffn_conv_w, conv_b3, w_down_bf16,
                       row(norm_post_ffn[i]), layer=i, tm=tm, tf=FFN_TF, rows_out=rows_out, name="conv_ffn")
    return xs.reshape(BATCH, SEQ, D_MODEL)
```

```python
import functools

import numpy as np
import jax
import jax.numpy as jnp
from jax import lax
from jax.experimental import pallas as pl
from jax.experimental.pallas import tpu as pltpu

F32 = jnp.float32
BF16 = jnp.bfloat16

D_MODEL = 2048
BATCH = 2
SEQ = 4096
DEPTH = 4
GRID_W = 64
CTX_LEN = 256
EPS = 1e-6
MOD_CHUNKS = 6
D_FF = 5632
GQA_HEADS = 16
GQA_KV_HEADS = 4
HEAD_DIM = 128
ROPE_THETA = 10000.0
MLSTM_HEADS = 8
MLSTM_DQK = 128
MLSTM_DV = 256
M_INIT = -1e30
CONV_WIDTH = 31
NA_HEADS = 16
NA_HEAD_DIM = 128
NA_ROWS = 8
NA_COLS = 16

R_LAT = BATCH * SEQ
R_ALL = R_LAT + BATCH * CTX_LEN
NEG = -1e30

VMEM_LIMIT = 56 * 1024 * 1024
SEG = 256
HALO = 16
FFN_SUB = 256
FFN_ROW_BLOCKS = 2
FFN_TF = 512
FFN_TM_LATENT = 1024


def _cparams(sem, flags=None):
    return pltpu.CompilerParams(dimension_semantics=sem, vmem_limit_bytes=VMEM_LIMIT, flags=flags)


def _sigmoid(x):
    return 1.0 / (1.0 + jnp.exp(-x))


def _rms(x, w):
    return x * lax.rsqrt(jnp.mean(x * x, axis=-1, keepdims=True) + EPS) * w


def _dot(a, b):
    return jnp.dot(a, b, preferred_element_type=F32)


def _dot_nt(a, b):
    return lax.dot_general(a, b, (((1,), (1,)), ((), ())), preferred_element_type=F32)


def _dot_tn(a, b):
    return lax.dot_general(a, b, (((0,), (0,)), ((), ())), preferred_element_type=F32)


def _is_seq_start(r):
    return (r & jnp.where(r < R_LAT, SEQ - 1, CTX_LEN - 1)) == 0


def _mod_index(tm):
    per_batch = SEQ // tm
    return lambda i: jnp.minimum(i // per_batch, BATCH)


def _mod_kernel(a_ref, w_ref, b_ref, o_ref):
    a = a_ref[...]
    s = (a * _sigmoid(a)).astype(BF16)
    o_ref[...] = _dot(s, w_ref[...].astype(BF16)) + b_ref[...]


def _modulation(a8, mod_w, mod_b):
    tn = 1024
    n = MOD_CHUNKS * D_MODEL
    return pl.pallas_call(
        _mod_kernel,
        grid=(DEPTH, n // tn),
        in_specs=[
            pl.BlockSpec((8, D_MODEL), lambda l, j: (0, 0)),
            pl.BlockSpec((None, D_MODEL, tn), lambda l, j: (l, 0, j)),
            pl.BlockSpec((None, 1, tn), lambda l, j: (l, 0, j)),
        ],
        out_specs=pl.BlockSpec((None, 8, tn), lambda l, j: (l, 0, j)),
        out_shape=jax.ShapeDtypeStruct((DEPTH, 8, n), F32),
        compiler_params=_cparams(("parallel", "parallel")),
        name="modulation",
    )(a8, mod_w, mod_b.reshape(DEPTH, 1, n))


def _norm_mod_to_scratch(x_ref, nw_ref, sh_ref, sc_ref, h_scr):
    @pl.when(pl.program_id(1) == 0)
    def _():
        h = _rms(x_ref[...], nw_ref[...]) * (1.0 + sc_ref[...]) + sh_ref[...]
        h_scr[...] = h.astype(BF16)


def _proj_plain_kernel(x_ref, nw_ref, sh_ref, sc_ref, w_ref, b_ref, o_ref, h_scr, *, n_scaled, scale):
    _norm_mod_to_scratch(x_ref, nw_ref, sh_ref, sc_ref, h_scr)
    y = _dot(h_scr[...], w_ref[...]) + b_ref[...]
    if n_scaled:
        y = y * jnp.where(pl.program_id(1) < n_scaled, scale, 1.0).astype(F32)
    o_ref[...] = y.astype(o_ref.dtype)


def _proj_mlstm_kernel(x_ref, nw_ref, sh_ref, sc_ref, w_ref, wg_ref, bg_ref, o_ref, gt_ref, h_scr):
    _norm_mod_to_scratch(x_ref, nw_ref, sh_ref, sc_ref, h_scr)

    @pl.when(pl.program_id(1) == 0)
    def _():
        gt_ref[...] = (_dot(h_scr[...], wg_ref[...]) + bg_ref[...]).T

    o_ref[...] = _dot(h_scr[...], w_ref[...]).astype(o_ref.dtype)


def _proj_glu_kernel(x_ref, nw_ref, sh_ref, sc_ref, wa_ref, wg_ref, ba_ref, bg_ref, o_ref, h_scr):
    _norm_mod_to_scratch(x_ref, nw_ref, sh_ref, sc_ref, h_scr)
    h = h_scr[...]
    a = _dot(h, wa_ref[...]) + ba_ref[...]
    g = _dot(h, wg_ref[...]) + bg_ref[...]
    o_ref[...] = (a * _sigmoid(g)).astype(o_ref.dtype)


def _rope(y, cos, sins, first_half):
    partner = jnp.where(first_half, pltpu.roll(y, 96, 1), pltpu.roll(y, 32, 1))
    return y * cos + partner * sins


def _proj_gqa_kernel(x_ref, nw_ref, sh_ref, sc_ref, w_ref, qn_ref, kn_ref, cos_ref, sin_ref, o_ref, h_scr):
    _norm_mod_to_scratch(x_ref, nw_ref, sh_ref, sc_ref, h_scr)
    j = pl.program_id(1)
    y = _dot(h_scr[...], w_ref[...])
    tn = y.shape[1]
    n_q = GQA_HEADS * HEAD_DIM // tn
    n_k = GQA_KV_HEADS * HEAD_DIM // tn

    def normed(nw, scale):
        cos = cos_ref[...]
        sins = sin_ref[...]
        lane = lax.broadcasted_iota(jnp.int32, cos.shape, 1)
        first_half = (lane & 63) < 32
        for s in range(tn // HEAD_DIM):
            ys = y[:, s * HEAD_DIM:(s + 1) * HEAD_DIM]
            ys = _rms(ys, nw) * scale
            o_ref[:, s * HEAD_DIM:(s + 1) * HEAD_DIM] = _rope(ys, cos, sins, first_half).astype(o_ref.dtype)

    @pl.when(j < n_q)
    def _():
        normed(qn_ref[...], HEAD_DIM ** -0.5)

    @pl.when(jnp.logical_and(j >= n_q, j < n_q + n_k))
    def _():
        normed(kn_ref[...], 1.0)

    @pl.when(j >= n_q + n_k)
    def _():
        o_ref[...] = y.astype(o_ref.dtype)


def _project(x, nw, sh, sc, w, *, mode, tm, tn, out_dtype=BF16, bias=None, n_scaled=0, scale=1.0,
             extra=None, name):
    rows = x.shape[0]
    n = w.shape[1]
    nt = rows // tm
    mi = _mod_index(tm)
    row_spec = pl.BlockSpec((tm, D_MODEL), lambda i, j: (i, 0))
    vec_spec = pl.BlockSpec((1, D_MODEL), lambda i, j: (0, 0))
    mod_spec = pl.BlockSpec((None, 1, D_MODEL), lambda i, j: (mi(i), 0, 0))
    common = [row_spec, vec_spec, mod_spec, mod_spec]
    args = [x, nw, sh, sc]
    if mode == "plain":
        n_out = n
        if bias is None:
            bias = jnp.zeros((1, n), F32)
        in_specs = common + [pl.BlockSpec((D_MODEL, tn), lambda i, j: (0, j)),
                             pl.BlockSpec((1, tn), lambda i, j: (0, j))]
        args += [w, bias]
        body = functools.partial(_proj_plain_kernel, n_scaled=n_scaled, scale=scale)
    elif mode == "glu":
        n_out = n // 2
        off = n_out // tn
        in_specs = common + [pl.BlockSpec((D_MODEL, tn), lambda i, j: (0, j)),
                             pl.BlockSpec((D_MODEL, tn), lambda i, j: (0, j + off)),
                             pl.BlockSpec((1, tn), lambda i, j: (0, j)),
                             pl.BlockSpec((1, tn), lambda i, j: (0, j + off))]
        args += [w, w, bias, bias]
        body = _proj_glu_kernel
    elif mode == "mlstm":
        n_out = n
        w_g, b_g = extra
        ng = w_g.shape[1]
        in_specs = common + [pl.BlockSpec((D_MODEL, tn), lambda i, j: (0, j)),
                             pl.BlockSpec((D_MODEL, ng), lambda i, j: (0, 0)),
                             pl.BlockSpec((1, ng), lambda i, j: (0, 0))]
        args += [w, w_g, b_g]
        return pl.pallas_call(
            _proj_mlstm_kernel,
            grid=(nt, n_out // tn),
            in_specs=in_specs,
            out_specs=[pl.BlockSpec((tm, tn), lambda i, j: (i, j)),
                       pl.BlockSpec((ng, tm), lambda i, j: (0, i))],
            out_shape=[jax.ShapeDtypeStruct((rows, n_out), out_dtype),
                       jax.ShapeDtypeStruct((ng, rows), F32)],
            scratch_shapes=[pltpu.VMEM((tm, D_MODEL), BF16)],
            compiler_params=_cparams(("parallel", "arbitrary")),
            name=name,
        )(*args)
    else:
        n_out = n
        qn, kn, cos, sins = extra
        head_spec = pl.BlockSpec((1, HEAD_DIM), lambda i, j: (0, 0))
        tab_spec = pl.BlockSpec((tm, HEAD_DIM), lambda i, j: (i, 0))
        in_specs = common + [pl.BlockSpec((D_MODEL, tn), lambda i, j: (0, j)),
                             head_spec, head_spec, tab_spec, tab_spec]
        args += [w, qn, kn, cos, sins]
        body = _proj_gqa_kernel
    return pl.pallas_call(
        body,
        grid=(nt, n_out // tn),
        in_specs=in_specs,
        out_specs=pl.BlockSpec((tm, tn), lambda i, j: (i, j)),
        out_shape=jax.ShapeDtypeStruct((rows, n_out), out_dtype),
        scratch_shapes=[pltpu.VMEM((tm, D_MODEL), BF16)],
        compiler_params=_cparams(("parallel", "arbitrary")),
        name=name,
    )(*args)


def _outproj_kernel(a_ref, w_ref, b_ref, x_ref, pw_ref, g_ref, o_ref):
    y = _dot(a_ref[...], w_ref[...]) + b_ref[...]
    o_ref[...] = x_ref[...] + g_ref[...] * _rms(y, pw_ref[...])


def _outproj_mlstm_kernel(hf_ref, hb_ref, og_ref, on_ref, w_ref, b_ref, x_ref, pw_ref, g_ref, o_ref, a_scr):
    for h in range(MLSTM_HEADS):
        sl = slice(h * MLSTM_DV, (h + 1) * MLSTM_DV)
        hh = hf_ref[:, sl] + hb_ref[:, sl]
        hh = _rms(hh, on_ref[:, sl]) * _sigmoid(og_ref[:, sl].astype(F32))
        a_scr[:, sl] = hh.astype(BF16)
    y = _dot(a_scr[...], w_ref[...]) + b_ref[...]
    o_ref[...] = x_ref[...] + g_ref[...] * _rms(y, pw_ref[...])


def _out_project(a, w, bias, x, pw, g, *, tm, rows_out, name, mlstm=None):
    nt = rows_out // tm
    mi = _mod_index(tm)
    row_spec = pl.BlockSpec((tm, D_MODEL), lambda i: (i, 0))
    vec_spec = pl.BlockSpec((1, D_MODEL), lambda i: (0, 0))
    w_spec = pl.BlockSpec((D_MODEL, D_MODEL), lambda i: (0, 0))
    mod_spec = pl.BlockSpec((None, 1, D_MODEL), lambda i: (mi(i), 0, 0))
    if bias is None:
        bias = jnp.zeros((1, D_MODEL), F32)
    if mlstm is None:
        body = _outproj_kernel
        in_specs = [row_spec, w_spec, vec_spec, row_spec, vec_spec, mod_spec]
        args = [a, w, bias, x, pw, g]
        scratch = []
    else:
        hdir, z, out_norm = mlstm
        body = _outproj_mlstm_kernel
        o_gate_block = (2 * MLSTM_HEADS * MLSTM_DQK + MLSTM_HEADS * MLSTM_DV) // D_MODEL
        in_specs = [pl.BlockSpec((None, tm, D_MODEL), lambda i: (0, i, 0)),
                    pl.BlockSpec((None, tm, D_MODEL), lambda i: (1, i, 0)),
                    pl.BlockSpec((tm, D_MODEL), lambda i: (i, o_gate_block)),
                    vec_spec, w_spec, vec_spec, row_spec, vec_spec, mod_spec]
        args = [hdir, hdir, z, out_norm, w, bias, x, pw, g]
        scratch = [pltpu.VMEM((tm, D_MODEL), BF16)]
    return pl.pallas_call(
        body,
        grid=(nt,),
        in_specs=in_specs,
        out_specs=row_spec,
        out_shape=jax.ShapeDtypeStruct((rows_out, D_MODEL), F32),
        scratch_shapes=scratch,
        compiler_params=_cparams(("parallel",)),
        name=name,
    )(*args)


def _ffn_kernel(x_ref, xp_ref, xn_ref, nw_ref, sh_ref, sc_ref, g_ref, wv_ref, wg_ref, cwv_ref, cwg_ref,
                cbv_ref, cbg_ref, wd_ref, pw_ref, o_ref, h_scr, uv_scr, ug_scr, acc_scr, *, tm, tile0):
    i = pl.program_id(0) + tile0
    f = pl.program_id(1)

    @pl.when(f == 0)
    def _():
        def nm(xr):
            return (_rms(xr[...], nw_ref[...]) * (1.0 + sc_ref[...]) + sh_ref[...]).astype(BF16)
        h_scr[0:HALO, :] = nm(xp_ref)
        h_scr[HALO:HALO + tm, :] = nm(x_ref)
        h_scr[HALO + tm:HALO + tm + HALO, :] = nm(xn_ref)
        acc_scr[...] = jnp.zeros_like(acc_scr)

    r = i * tm + lax.broadcasted_iota(jnp.int32, (tm, 1), 0)
    has_left = jnp.where(_is_seq_start(r), 0.0, 1.0)
    has_right = jnp.where(_is_seq_start(r + 1), 0.0, 1.0)

    n_rh, n_sub = uv_scr.shape[0], uv_scr.shape[1]
    hr = tm // n_rh

    def conv(u_scr, cw_ref, cb_ref, cl, rows):
        left = u_scr[HALO - 1:HALO - 1 + hr, :] * has_left[rows]
        right = u_scr[HALO + 1:HALO + 1 + hr, :] * has_right[rows]
        return (left * cw_ref[0:1, cl] + u_scr[HALO:HALO + hr, :] * cw_ref[1:2, cl]
                + right * cw_ref[2:3, cl] + cb_ref[:, cl])

    for rh in range(n_rh):
        hh = h_scr[rh * hr:rh * hr + hr + 2 * HALO, :]
        for cb in range(n_sub):
            cl = slice(cb * FFN_SUB, (cb + 1) * FFN_SUB)
            uv_scr[rh, cb] = _dot(hh, wv_ref[:, cl])
            ug_scr[rh, cb] = _dot(hh, wg_ref[:, cl])
    for rh in range(n_rh):
        rows = slice(rh * hr, (rh + 1) * hr)
        acts = []
        for cb in range(n_sub):
            cl = slice(cb * FFN_SUB, (cb + 1) * FFN_SUB)
            val = conv(uv_scr.at[rh, cb], cwv_ref, cbv_ref, cl, rows)
            gate = conv(ug_scr.at[rh, cb], cwg_ref, cbg_ref, cl, rows)
            acts.append((val * (gate * _sigmoid(gate))).astype(BF16))
        acc_scr[rows, :] += _dot(jnp.concatenate(acts, axis=1), wd_ref[...])

    @pl.when(f == pl.num_programs(1) - 1)
    def _():
        o_ref[...] = x_ref[...] + g_ref[...] * _rms(acc_scr[...], pw_ref[...])


def _ffn_kernel_into(*refs, n_in, **kw):
    _ffn_kernel(*refs[:n_in], *refs[n_in + 1:], **kw)


def _conv_ffn(x, nw, sh, sc, g, w_up, conv_w, conv_b, w_down, pw, *, layer, tm, tf, rows_out, name,
              tile0=0, n_tiles=None, into=None):
    rows = x.shape[0]
    nt = rows_out // tm if n_tiles is None else n_tiles
    nf = D_FF // tf
    mi = _mod_index(tm)
    hb = tm // HALO
    last_halo = rows // HALO - 1
    once = pl.Buffered(1) if tm > 512 else None
    vec_spec = pl.BlockSpec((1, D_MODEL), lambda i, f: (0, 0))
    mod_spec = pl.BlockSpec((None, 1, D_MODEL), lambda i, f: (mi(i + tile0), 0, 0))
    in_specs = [
        pl.BlockSpec((tm, D_MODEL), lambda i, f: (i + tile0, 0), pipeline_mode=once),
        pl.BlockSpec((HALO, D_MODEL), lambda i, f: (jnp.maximum((i + tile0) * hb - 1, 0), 0)),
        pl.BlockSpec((HALO, D_MODEL), lambda i, f: (jnp.minimum((i + tile0 + 1) * hb, last_halo), 0)),
        vec_spec, mod_spec, mod_spec, mod_spec,
        pl.BlockSpec((None, D_MODEL, tf), lambda i, f: (layer, 0, f)),
        pl.BlockSpec((None, D_MODEL, tf), lambda i, f: (layer, 0, f + nf)),
        pl.BlockSpec((None, 3, tf), lambda i, f: (layer, 0, f)),
        pl.BlockSpec((None, 3, tf), lambda i, f: (layer, 0, f + nf)),
        pl.BlockSpec((None, 1, tf), lambda i, f: (layer, 0, f)),
        pl.BlockSpec((None, 1, tf), lambda i, f: (layer, 0, f + nf)),
        pl.BlockSpec((None, tf, D_MODEL), lambda i, f: (layer, f, 0)),
        vec_spec,
    ]
    args = [x, x, x, nw, sh, sc, g, w_up, w_up, conv_w, conv_w, conv_b, conv_b, w_down, pw]
    body = functools.partial(_ffn_kernel, tm=tm, tile0=tile0)
    aliases = {}
    if into is not None:
        body = functools.partial(_ffn_kernel_into, n_in=len(args), tm=tm, tile0=tile0)
        aliases = {len(args): 0}
        in_specs = in_specs + [pl.BlockSpec(memory_space=pl.ANY)]
        args = args + [into]
    return pl.pallas_call(
        body,
        grid=(nt, nf),
        in_specs=in_specs,
        out_specs=pl.BlockSpec((tm, D_MODEL), lambda i, f: (i + tile0, 0), pipeline_mode=once),
        out_shape=jax.ShapeDtypeStruct((rows_out, D_MODEL), F32),
        input_output_aliases=aliases,
        scratch_shapes=[pltpu.VMEM((tm + 2 * HALO, D_MODEL), BF16),
                        pltpu.VMEM((FFN_ROW_BLOCKS, tf // FFN_SUB, tm // FFN_ROW_BLOCKS + 2 * HALO, FFN_SUB), F32),
                        pltpu.VMEM((FFN_ROW_BLOCKS, tf // FFN_SUB, tm // FFN_ROW_BLOCKS + 2 * HALO, FFN_SUB), F32),
                        pltpu.VMEM((tm, D_MODEL), F32)],
        compiler_params=_cparams(("parallel", "arbitrary")),
        name=name,
    )(*args)


def _exp_bf16(x):
    return jnp.exp(x.astype(BF16))


def _with_ones(v):
    return jnp.concatenate([v, jnp.ones_like(v)], axis=1)


GQA_SCORES_AHEAD = 2


def _gqa_attn_kernel(q_ref, kl_ref, vl_ref, kc_ref, vc_ref, o_ref, vla_scr, vca_scr):
    t = pl.program_id(2)
    n_lat_tiles = pl.num_programs(2) - 1
    group = GQA_HEADS // GQA_KV_HEADS
    kc = kc_ref[...]

    @pl.when(t == 0)
    def _():
        vla_scr[...] = _with_ones(vl_ref[...])
        vca_scr[...] = _with_ones(vc_ref[...])

    @pl.when(t < n_lat_tiles)
    def _():
        kl = kl_ref[...]

        def scores(gi):
            q = q_ref[:, gi * HEAD_DIM:(gi + 1) * HEAD_DIM]
            return _dot_nt(q, kc), _dot_nt(q, kl)

        ahead = GQA_SCORES_AHEAD
        pending = [scores(gi) for gi in range(min(ahead, group))]
        for gi in range(group):
            sl = slice(gi * HEAD_DIM, (gi + 1) * HEAD_DIM)
            s_c, s_l = pending.pop(0)
            if gi + ahead < group:
                pending.append(scores(gi + ahead))
            m = jnp.maximum(jnp.max(s_c, axis=1, keepdims=True), jnp.max(s_l, axis=1, keepdims=True))
            p_c = _exp_bf16(s_c - m)
            p_l = _exp_bf16(s_l - m)
            hq = p_l.shape[0] // 2
            for rs in (slice(0, hq), slice(hq, 2 * hq)):
                o = _dot(p_c[rs], vca_scr[...]) + _dot(p_l[rs], vla_scr[...])
                o_ref[rs, sl] = (o[:, :HEAD_DIM] / o[:, HEAD_DIM:HEAD_DIM + 1]).astype(o_ref.dtype)

    @pl.when(t == n_lat_tiles)
    def _():
        for gi in range(group):
            sl = slice(gi * HEAD_DIM, (gi + 1) * HEAD_DIM)
            q = q_ref[:, sl]
            s_c = _dot_nt(q, kc)
            p_c = _exp_bf16(s_c - jnp.max(s_c, axis=1, keepdims=True))
            o = _dot(p_c, vca_scr[...])
            o_ref[:, sl] = (o[:, :HEAD_DIM] / o[:, HEAD_DIM:HEAD_DIM + 1]).astype(o_ref.dtype)


def _gqa_attention(z):
    tq = SEG
    n_lat = SEQ // tq
    group_w = (GQA_HEADS // GQA_KV_HEADS) * HEAD_DIM
    k_col = GQA_HEADS
    v_col = GQA_HEADS + GQA_KV_HEADS
    ctx_blk = R_LAT // SEG

    def q_rows(b, kh, t):
        return jnp.where(t < n_lat, b * n_lat + t, ctx_blk + b)

    return pl.pallas_call(
        _gqa_attn_kernel,
        grid=(BATCH, GQA_KV_HEADS, n_lat + 1),
        in_specs=[
            pl.BlockSpec((tq, group_w), lambda b, kh, t: (q_rows(b, kh, t), kh)),
            pl.BlockSpec((SEQ, HEAD_DIM), lambda b, kh, t: (b, k_col + kh)),
            pl.BlockSpec((SEQ, HEAD_DIM), lambda b, kh, t: (b, v_col + kh)),
            pl.BlockSpec((CTX_LEN, HEAD_DIM), lambda b, kh, t: (ctx_blk + b, k_col + kh)),
            pl.BlockSpec((CTX_LEN, HEAD_DIM), lambda b, kh, t: (ctx_blk + b, v_col + kh)),
        ],
        out_specs=pl.BlockSpec((tq, group_w), lambda b, kh, t: (q_rows(b, kh, t), kh)),
        out_shape=jax.ShapeDtypeStruct((R_ALL, GQA_HEADS * HEAD_DIM), BF16),
        scratch_shapes=[pltpu.VMEM((SEQ, 2 * HEAD_DIM), BF16),
                        pltpu.VMEM((CTX_LEN, 2 * HEAD_DIM), BF16)],
        compiler_params=_cparams(("parallel", "parallel", "arbitrary")),
        name="gqa_attention",
    )(z, z, z, z, z)


def _split3(x):
    hi = x.astype(BF16)
    r1 = x - hi.astype(F32)
    mid = r1.astype(BF16)
    lo = (r1 - mid.astype(F32)).astype(BF16)
    return hi, mid, lo


def _log_sigmoid(x):
    return jnp.minimum(x, 0.0) - jnp.log(1.0 + jnp.exp(-jnp.abs(x)))


MLSTM_AUG = MLSTM_DV + 128


def _mlstm_kernel(q_ref, k_ref, v_ref, g_ref, o_ref, c_scr, m_scr):
    d = pl.program_id(1)
    c = pl.program_id(2)
    nh = MLSTM_HEADS
    L = SEG
    scale = MLSTM_DQK ** -0.5

    @pl.when(c == 0)
    def _():
        c_scr[...] = jnp.zeros_like(c_scr)
        m_scr[...] = jnp.full_like(m_scr, M_INIT)

    ri = lax.broadcasted_iota(jnp.int32, (L, L), 0)
    ci = lax.broadcasted_iota(jnp.int32, (L, L), 1)
    diff = jnp.where(d == 0, ci - ri, ri - ci)
    sees = diff <= 0
    cum_t = jnp.where(diff >= 0, 1.0, 0.0).astype(BF16)

    g = g_ref[...]
    ig = g[0:nh]
    lf = _log_sigmoid(g[nh:2 * nh])
    b = sum(_dot(p, cum_t) for p in _split3(lf))
    g_tot = jnp.sum(lf, axis=1, keepdims=True)
    m_in = m_scr[:, 0:1]
    cr = ig - b
    lane = lax.broadcasted_iota(jnp.int32, (nh, L), 1)
    pm = cr
    for step in range(L.bit_length() - 1):
        sh = 1 << step
        prev = jnp.where(lane >= sh, pltpu.roll(pm, sh, 1), NEG)
        nxt = jnp.where(lane < L - sh, pltpu.roll(pm, L - sh, 1), NEG)
        pm = jnp.maximum(pm, jnp.where(d == 0, prev, nxt))
    big_m = jnp.maximum(m_in, pm)
    w_inter = jnp.exp(m_in - big_m) * scale
    floor = jnp.exp(-(b + big_m))
    a = g_tot - b + ig
    m_loc = jnp.max(a, axis=1, keepdims=True)
    w = jnp.exp(a - m_loc)
    m_new = jnp.maximum(g_tot + m_in, m_loc)
    a_old = jnp.exp(g_tot + m_in - m_new)
    a_new = jnp.exp(m_loc - m_new)
    slab = jnp.concatenate([big_m, w_inter, floor, jnp.zeros((128 - 3 * nh, L), F32)], axis=0)
    cols = slab.T

    ones = jnp.ones((L, MLSTM_AUG - MLSTM_DV), BF16)
    for h in range(nh):
        q = q_ref[:, h * MLSTM_DQK:(h + 1) * MLSTM_DQK]
        k = k_ref[:, h * MLSTM_DQK:(h + 1) * MLSTM_DQK]
        v_aug = jnp.concatenate([v_ref[:, h * MLSTM_DV:(h + 1) * MLSTM_DV], ones], axis=1)
        e = jnp.where(sees, cr[h:h + 1, :] - cols[:, h:h + 1], NEG)
        s = _dot_nt(q, k) * (jnp.exp(e) * scale)
        c_in = c_scr[h]
        num = _dot(s.astype(BF16), v_aug) + cols[:, nh + h:nh + h + 1] * _dot(q, c_in.astype(BF16))
        den = jnp.maximum(jnp.abs(num[:, MLSTM_DV:MLSTM_DV + 1]), cols[:, 2 * nh + h:2 * nh + h + 1])
        o_ref[:, h * MLSTM_DV:(h + 1) * MLSTM_DV] = num[:, :MLSTM_DV] / den

        kw_t = (k.astype(F32).T * w[h:h + 1, :]).astype(BF16)
        c_scr[h] = a_old[h:h + 1, :] * c_in + a_new[h:h + 1, :] * _dot(kw_t, v_aug)
    m_scr[...] = jnp.broadcast_to(m_new, m_scr.shape)


def _mlstm(z, gates_t):
    n_lat = SEQ // SEG
    ctx_blk = R_LAT // SEG
    nqk = MLSTM_HEADS * MLSTM_DQK
    nv = MLSTM_HEADS * MLSTM_DV

    def rb(b, d, c):
        lat = jnp.where(d == 0, c - 1, n_lat - c)
        return jnp.where(c == 0, ctx_blk + b, b * n_lat + lat)

    return pl.pallas_call(
        _mlstm_kernel,
        grid=(BATCH, 2, n_lat + 1),
        in_specs=[
            pl.BlockSpec((SEG, nqk), lambda b, d, c: (rb(b, d, c), 0)),
            pl.BlockSpec((SEG, nqk), lambda b, d, c: (rb(b, d, c), 1)),
            pl.BlockSpec((SEG, nv), lambda b, d, c: (rb(b, d, c), 2 * nqk // nv)),
            pl.BlockSpec((2 * MLSTM_HEADS, SEG), lambda b, d, c: (d, rb(b, d, c))),
        ],
        out_specs=pl.BlockSpec((None, SEG, nv), lambda b, d, c: (d, rb(b, d, c), 0)),
        out_shape=jax.ShapeDtypeStruct((2, R_ALL, nv), F32),
        scratch_shapes=[pltpu.VMEM((MLSTM_HEADS, MLSTM_DQK, MLSTM_AUG), F32),
                        pltpu.VMEM((MLSTM_HEADS, 128), F32)],
        compiler_params=_cparams(("parallel", "parallel", "arbitrary")),
        name="mlstm",
    )(z, z, z, gates_t)


DW_ROWS = 32
DW_COLS = 512


def _dwconv_ln_kernel(u_ref, up_ref, un_ref, w_ref, b_ref, lw_ref, lb_ref, o_ref, pad_scr, conv_scr, sh_scr):
    i = pl.program_id(0)
    tm = u_ref.shape[0]
    r0 = i * tm
    keep_prev = jnp.logical_not(_is_seq_start(r0))
    keep_next = jnp.logical_not(_is_seq_start(r0 + tm))
    pad_scr[0:HALO, :] = jnp.where(keep_prev, up_ref[...].astype(F32), 0.0)
    pad_scr[HALO:HALO + tm, :] = u_ref[...].astype(F32)
    pad_scr[HALO + tm:HALO + tm + HALO, :] = jnp.where(keep_next, un_ref[...].astype(F32), 0.0)

    rc = DW_ROWS
    cw = DW_COLS
    half = CONV_WIDTH // 2
    n_sh = sh_scr.shape[1]

    def chunk(cs, carry):
        cl = pl.ds(pl.multiple_of(cs * cw, cw), cw)
        for s in range(1, 8):
            sh_scr[s] = pad_scr[s:s + n_sh, cl]
        for ib in range(tm // rc):
            base = ib * rc
            acc = jnp.zeros((rc, cw), F32) + b_ref[:, cl]
            for kk in range(CONV_WIDTH):
                al, s = divmod(HALO - half + kk, 8)
                off = base + 8 * al
                tap = pad_scr[off:off + rc, cl] if s == 0 else sh_scr[s, off:off + rc, :]
                acc = acc + tap * w_ref[kk:kk + 1, cl]
            conv_scr[base:base + rc, cl] = acc
        return carry

    lax.fori_loop(0, D_MODEL // cw, chunk, 0)
    y = conv_scr[...]
    mu = jnp.mean(y, axis=-1, keepdims=True)
    yc = y - mu
    var = jnp.mean(yc * yc, axis=-1, keepdims=True)
    yn = yc * lax.rsqrt(var + EPS) * lw_ref[...] + lb_ref[...]
    o_ref[...] = (yn * _sigmoid(yn)).astype(o_ref.dtype)


def _dwconv_ln_swish(u, w_dw, b_dw, ln_w, ln_b):
    rows = u.shape[0]
    tm = SEG
    hb = tm // HALO
    last_halo = rows // HALO - 1
    vec_spec = pl.BlockSpec((1, D_MODEL), lambda i: (0, 0))
    return pl.pallas_call(
        _dwconv_ln_kernel,
        grid=(rows // tm,),
        in_specs=[
            pl.BlockSpec((tm, D_MODEL), lambda i: (i, 0)),
            pl.BlockSpec((HALO, D_MODEL), lambda i: (jnp.maximum(i * hb - 1, 0), 0)),
            pl.BlockSpec((HALO, D_MODEL), lambda i: (jnp.minimum((i + 1) * hb, last_halo), 0)),
            pl.BlockSpec((CONV_WIDTH, D_MODEL), lambda i: (0, 0)),
            vec_spec, vec_spec, vec_spec,
        ],
        out_specs=pl.BlockSpec((tm, D_MODEL), lambda i: (i, 0)),
        out_shape=jax.ShapeDtypeStruct((rows, D_MODEL), BF16),
        scratch_shapes=[pltpu.VMEM((tm + 2 * HALO, D_MODEL), F32),
                        pltpu.VMEM((tm, D_MODEL), F32),
                        pltpu.VMEM((8, tm + 2 * HALO - 8, DW_COLS), F32)],
        compiler_params=_cparams(("parallel",)),
        name="dwconv_ln_swish",
    )(u, u, u, w_dw, b_dw, ln_w, ln_b)


NAT_Q_ROWS = 4
NAT_BAND_ROWS = 12
NAT_TQ = NAT_Q_ROWS * GRID_W
NAT_BAND = NAT_BAND_ROWS * GRID_W


NAT_HEADS_PER_STEP = 4


def _nat_kernel(q_ref, kl_ref, vl_ref, kc_ref, vc_ref, bias_ref, o_ref, vla_scr, vca_scr):
    gq = pl.program_id(2)
    max_start = (SEQ - NAT_BAND) // NAT_TQ
    start = pl.multiple_of(jnp.clip(gq - 1, 0, max_start) * NAT_TQ, NAT_TQ)
    hd = NA_HEAD_DIM

    @pl.when(gq == 0)
    def _():
        for hh in range(NAT_HEADS_PER_STEP):
            sl = slice(hh * hd, (hh + 1) * hd)
            vla_scr[:, 2 * hh * hd:2 * (hh + 1) * hd] = _with_ones(vl_ref[:, sl])
            vca_scr[:, 2 * hh * hd:2 * (hh + 1) * hd] = _with_ones(vc_ref[:, sl])

    def scores(hh):
        sl = slice(hh * hd, (hh + 1) * hd)
        q = q_ref[:, sl]
        return _dot_nt(q, kl_ref[pl.ds(start, NAT_BAND), sl]) + bias_ref[hh], _dot_nt(q, kc_ref[:, sl])

    s_next = scores(0)
    for hh in range(NAT_HEADS_PER_STEP):
        sl = slice(hh * hd, (hh + 1) * hd)
        sl2 = slice(2 * hh * hd, 2 * (hh + 1) * hd)
        s_n, s_c = s_next
        if hh + 1 < NAT_HEADS_PER_STEP:
            s_next = scores(hh + 1)
        m = jnp.maximum(jnp.max(s_n, axis=1, keepdims=True), jnp.max(s_c, axis=1, keepdims=True))
        p_n = _exp_bf16(s_n - m)
        p_c = _exp_bf16(s_c - m)
        o = _dot(p_n, vla_scr[pl.ds(start, NAT_BAND), sl2]) + _dot(p_c, vca_scr[:, sl2])
        o_ref[:, sl] = (o[:, :hd] / o[:, hd:hd + 1]).astype(o_ref.dtype)


def _nat_bias_blocks():
    rows = SEQ // GRID_W
    n_groups = rows // NAT_Q_ROWS
    q_col = np.arange(GRID_W)
    c_start = np.clip(q_col - NA_COLS // 2, 0, GRID_W - NA_COLS)
    dc = q_col[None, :] - c_start[:, None]
    col_ok = (dc >= 0) & (dc < NA_COLS)
    masked = 2 * NA_ROWS - 1
    block = np.full((3, NAT_Q_ROWS, NAT_BAND_ROWS), masked, np.int64)
    for var, gq in enumerate((0, 1, n_groups - 1)):
        lo = int(np.clip(gq - 1, 0, (SEQ - NAT_BAND) // NAT_TQ)) * NAT_Q_ROWS
        for qr in range(NAT_Q_ROWS):
            q_row = gq * NAT_Q_ROWS + qr
            r_start = int(np.clip(q_row - NA_ROWS // 2, 0, rows - NA_ROWS))
            for kr in range(NAT_BAND_ROWS):
                k_row = lo + kr
                if 0 <= k_row - r_start < NA_ROWS:
                    block[var, qr, kr] = k_row - q_row + (NA_ROWS - 1)
    return col_ok, block


def _nat_bias(rpb):
    n_r, n_c = 2 * NA_ROWS - 1, 2 * NA_COLS - 1
    w = GRID_W
    col_ok, block = _nat_bias_blocks()
    lead = w - NA_COLS
    vext = jnp.pad(rpb, ((0, 0), (0, 0), (lead, 2 * w - lead - n_c)))
    skew = jnp.broadcast_to(vext[:, :, None, :], (NA_HEADS, n_r, w, 2 * w)).reshape(NA_HEADS, n_r, 2 * w * w)
    skew = skew[:, :, :w * (2 * w - 1)].reshape(NA_HEADS, n_r, w, 2 * w - 1)
    tb = jnp.where(col_ok[None, None], skew[..., w - 1:], NEG)
    tb = jnp.concatenate([tb, jnp.full((NA_HEADS, 1, w, w), NEG, F32)], axis=1)
    strips = [jnp.concatenate([tb[:, int(d)] for d in block[var, qr]], axis=-1)
              for var in range(3) for qr in range(NAT_Q_ROWS)]
    return jnp.stack(strips, axis=1).reshape(NA_HEADS, 3, NAT_TQ, NAT_BAND)


def _nat_attention(z, rpb):
    n_groups = SEQ // NAT_TQ
    ctx_blk = R_LAT // SEG
    bias = _nat_bias(rpb)
    hps = NAT_HEADS_PER_STEP
    gw = hps * NA_HEAD_DIM

    def variant(gq):
        return (gq > 0).astype(jnp.int32) + (gq == n_groups - 1).astype(jnp.int32)

    k_col = NA_HEADS // hps
    v_col = 2 * NA_HEADS // hps
    return pl.pallas_call(
        _nat_kernel,
        grid=(BATCH, NA_HEADS // hps, n_groups),
        in_specs=[
            pl.BlockSpec((NAT_TQ, gw), lambda b, h, gq: (b * n_groups + gq, h)),
            pl.BlockSpec((SEQ, gw), lambda b, h, gq: (b, k_col + h)),
            pl.BlockSpec((SEQ, gw), lambda b, h, gq: (b, v_col + h)),
            pl.BlockSpec((CTX_LEN, gw), lambda b, h, gq: (ctx_blk + b, k_col + h)),
            pl.BlockSpec((CTX_LEN, gw), lambda b, h, gq: (ctx_blk + b, v_col + h)),
            pl.BlockSpec((hps, None, NAT_TQ, NAT_BAND), lambda b, h, gq: (h, variant(gq), 0, 0)),
        ],
        out_specs=pl.BlockSpec((NAT_TQ, gw), lambda b, h, gq: (b * n_groups + gq, h)),
        out_shape=jax.ShapeDtypeStruct((R_LAT, NA_HEADS * NA_HEAD_DIM), BF16),
        scratch_shapes=[pltpu.VMEM((SEQ, 2 * gw), BF16),
                        pltpu.VMEM((CTX_LEN, 2 * gw), BF16)],
        compiler_params=_cparams(("parallel", "parallel", "arbitrary")),
        name="nat_attention",
    )(z, z, z, z, z, bias)


def _rope_tables():
    pos = jnp.arange(SEQ)
    quarter = HEAD_DIM // 4
    freqs = ROPE_THETA ** (-jnp.arange(quarter, dtype=F32) / quarter)

    def cs(p):
        ang = p.astype(F32)[:, None] * freqs[None, :]
        return jnp.cos(ang), jnp.sin(ang)

    cr, sr = cs(pos // GRID_W)
    cc, sc = cs(pos % GRID_W)
    cos = jnp.concatenate([cr, cr, cc, cc], axis=1)
    sins = jnp.concatenate([-sr, sr, -sc, sc], axis=1)
    n_ctx = BATCH * CTX_LEN
    cos = jnp.concatenate([cos] * BATCH + [jnp.ones((n_ctx, HEAD_DIM), F32)], axis=0)
    sins = jnp.concatenate([sins] * BATCH + [jnp.zeros((n_ctx, HEAD_DIM), F32)], axis=0)
    return cos, sins


def kernel(x, c, ctx, c_ctx, mod_w, mod_b, norm_pre_mix, norm_post_mix, norm_pre_ffn, norm_post_ffn, ffn_w_up, ffn_conv_w, ffn_conv_b, ffn_w_down, gqa_w_in, gqa_q_norm, gqa_k_norm, gqa_w_out, mlstm_w_in, mlstm_b_gate, mlstm_out_norm, mlstm_w_out, conv_w_pw1, conv_b_pw1, conv_w_dw, conv_b_dw, conv_ln_w, conv_ln_b, conv_w_pw2, conv_b_pw2, nat_w_in, nat_rpb, nat_w_out):
    tm = 512
    xs = jnp.concatenate([x.reshape(R_LAT, D_MODEL), ctx.reshape(BATCH * CTX_LEN, D_MODEL)], axis=0)
    a8 = jnp.concatenate([c, c_ctx[None, :], jnp.zeros((8 - BATCH - 1, D_MODEL), F32)], axis=0)
    mods = _modulation(a8, mod_w, mod_b)
    mods = mods[:, :BATCH + 1].reshape(DEPTH, BATCH + 1, MOD_CHUNKS, 1, D_MODEL)
    cos, sins = _rope_tables()
    w_up_bf16 = ffn_w_up.astype(BF16)
    w_down_bf16 = ffn_w_down.astype(BF16)
    conv_b3 = ffn_conv_b.reshape(DEPTH, 1, 2 * D_FF)

    def row(v):
        return v.reshape(1, -1)

    for i in range(DEPTH):
        kind = i % 4
        last = i == DEPTH - 1
        rows_out = R_LAT if last else R_ALL
        sh1, sc1, g1, sh2, sc2, g2 = [mods[i, :, m] for m in range(MOD_CHUNKS)]
        pre = row(norm_pre_mix[i])
        post = row(norm_post_mix[i])
        if kind == 0:
            z = _project(xs, pre, sh1, sc1, gqa_w_in[0].astype(BF16), mode="gqa", tm=tm, tn=512,
                         extra=(row(gqa_q_norm[0]), row(gqa_k_norm[0]), cos, sins), name="gqa_in")
            a = _gqa_attention(z)
            xs = _out_project(a, gqa_w_out[0].astype(BF16), None, xs, post, g1, tm=tm, rows_out=rows_out,
                              name="gqa_out")
        elif kind == 1:
            n_main = 2 * MLSTM_HEADS * MLSTM_DQK + 2 * MLSTM_HEADS * MLSTM_DV
            n_gate = 4 * MLSTM_HEADS
            w_in = mlstm_w_in[0]
            w_g = jnp.pad(w_in[:, n_main:], ((0, 0), (0, 128 - n_gate))).astype(BF16)
            b_g = jnp.pad(mlstm_b_gate[0], (0, 128 - n_gate)).reshape(1, 128)
            z, gates_t = _project(xs, pre, sh1, sc1, w_in[:, :n_main].astype(BF16), mode="mlstm", tm=tm, tn=2048,
                                  extra=(w_g, b_g), name="mlstm_in")
            hdir = _mlstm(z, gates_t)
            xs = _out_project(None, mlstm_w_out[0].astype(BF16), None, xs, post, g1, tm=SEG, rows_out=rows_out,
                              name="mlstm_out", mlstm=(hdir, z, row(mlstm_out_norm[0])))
        elif kind == 2:
            u = _project(xs, pre, sh1, sc1, conv_w_pw1[0].astype(BF16), mode="glu", tm=tm, tn=1024,
                         bias=row(conv_b_pw1[0]), name="conformer_pw1")
            a = _dwconv_ln_swish(u, conv_w_dw[0], row(conv_b_dw[0]), row(conv_ln_w[0]), row(conv_ln_b[0]))
            xs = _out_project(a, conv_w_pw2[0].astype(BF16), row(conv_b_pw2[0]), xs, post, g1, tm=tm,
                              rows_out=rows_out, name="conformer_pw2")
        else:
            nq = NA_HEADS * NA_HEAD_DIM
            z = _project(xs, pre, sh1, sc1, nat_w_in[0].astype(BF16), mode="plain", tm=tm, tn=2048,
                         n_scaled=nq // 2048, scale=NA_HEAD_DIM ** -0.5, name="nat_in")
            a = _nat_attention(z, nat_rpb[0])
            xs = _out_project(a, nat_w_out[0].astype(BF16), None, xs, post, g1, tm=tm, rows_out=rows_out,
                              name="nat_out")
        ffn_args = (xs, row(norm_pre_ffn[i]), sh2, sc2, g2, w_up_bf16, ffn_conv_w, conv_b3, w_down_bf16,
                    row(norm_post_ffn[i]))
        x_new = _conv_ffn(*ffn_args, layer=i, tm=FFN_TM_LATENT, tf=FFN_TF, rows_out=rows_out,
                          n_tiles=R_LAT // FFN_TM_LATENT, name="conv_ffn")
        if not last:
            x_new = _conv_ffn(*ffn_args, layer=i, tm=tm, tf=FFN_TF, rows_out=rows_out, tile0=R_LAT // tm,
                              n_tiles=(R_ALL - R_LAT) // tm, into=x_new, name="conv_ffn_ctx")
        xs = x_new
    return xs.reshape(BATCH, SEQ, D_MODEL)
```

```python
import functools

import numpy as np
import jax
import jax.numpy as jnp
from jax import lax
from jax.experimental import pallas as pl
from jax.experimental.pallas import tpu as pltpu

F32 = jnp.float32
BF16 = jnp.bfloat16

D_MODEL = 2048
BATCH = 2
SEQ = 4096
DEPTH = 4
GRID_W = 64
CTX_LEN = 256
EPS = 1e-6
MOD_CHUNKS = 6
D_FF = 5632
GQA_HEADS = 16
GQA_KV_HEADS = 4
HEAD_DIM = 128
ROPE_THETA = 10000.0
MLSTM_HEADS = 8
MLSTM_DQK = 128
MLSTM_DV = 256
M_INIT = -1e30
CONV_WIDTH = 31
NA_HEADS = 16
NA_HEAD_DIM = 128
NA_ROWS = 8
NA_COLS = 16

R_LAT = BATCH * SEQ
R_ALL = R_LAT + BATCH * CTX_LEN
NEG = -1e30

VMEM_LIMIT = 56 * 1024 * 1024
SEG = 256
HALO = 16
FFN_SUB = 256
FFN_ROW_BLOCKS = 2
FFN_TF = 512
FFN_TM_LATENT = 1024


def _cparams(sem, flags=None):
    return pltpu.CompilerParams(dimension_semantics=sem, vmem_limit_bytes=VMEM_LIMIT, flags=flags)


def _sigmoid(x):
    return 1.0 / (1.0 + jnp.exp(-x))


def _rms(x, w):
    return x * lax.rsqrt(jnp.mean(x * x, axis=-1, keepdims=True) + EPS) * w


def _dot(a, b):
    return jnp.dot(a, b, preferred_element_type=F32)


def _dot_nt(a, b):
    return lax.dot_general(a, b, (((1,), (1,)), ((), ())), preferred_element_type=F32)


def _dot_tn(a, b):
    return lax.dot_general(a, b, (((0,), (0,)), ((), ())), preferred_element_type=F32)


def _is_seq_start(r):
    return (r & jnp.where(r < R_LAT, SEQ - 1, CTX_LEN - 1)) == 0


def _mod_index(tm):
    per_batch = SEQ // tm
    return lambda i: jnp.minimum(i // per_batch, BATCH)


def _mod_kernel(a_ref, w_ref, b_ref, o_ref):
    a = a_ref[...]
    s = (a * _sigmoid(a)).astype(BF16)
    o_ref[...] = _dot(s, w_ref[...].astype(BF16)) + b_ref[...]


def _modulation(a8, mod_w, mod_b):
    tn = 1024
    n = MOD_CHUNKS * D_MODEL
    return pl.pallas_call(
        _mod_kernel,
        grid=(DEPTH, n // tn),
        in_specs=[
            pl.BlockSpec((8, D_MODEL), lambda l, j: (0, 0)),
            pl.BlockSpec((None, D_MODEL, tn), lambda l, j: (l, 0, j)),
            pl.BlockSpec((None, 1, tn), lambda l, j: (l, 0, j)),
        ],
        out_specs=pl.BlockSpec((None, 8, tn), lambda l, j: (l, 0, j)),
        out_shape=jax.ShapeDtypeStruct((DEPTH, 8, n), F32),
        compiler_params=_cparams(("parallel", "parallel")),
        name="modulation",
    )(a8, mod_w, mod_b.reshape(DEPTH, 1, n))


def _norm_mod_to_scratch(x_ref, nw_ref, sh_ref, sc_ref, h_scr):
    @pl.when(pl.program_id(1) == 0)
    def _():
        h = _rms(x_ref[...], nw_ref[...]) * (1.0 + sc_ref[...]) + sh_ref[...]
        h_scr[...] = h.astype(BF16)


def _proj_plain_kernel(x_ref, nw_ref, sh_ref, sc_ref, w_ref, b_ref, o_ref, h_scr, *, n_scaled, scale):
    _norm_mod_to_scratch(x_ref, nw_ref, sh_ref, sc_ref, h_scr)
    y = _dot(h_scr[...], w_ref[...]) + b_ref[...]
    if n_scaled:
        y = y * jnp.where(pl.program_id(1) < n_scaled, scale, 1.0).astype(F32)
    o_ref[...] = y.astype(o_ref.dtype)


def _proj_mlstm_kernel(x_ref, nw_ref, sh_ref, sc_ref, w_ref, wg_ref, bg_ref, o_ref, gt_ref, h_scr):
    _norm_mod_to_scratch(x_ref, nw_ref, sh_ref, sc_ref, h_scr)

    @pl.when(pl.program_id(1) == 0)
    def _():
        gt_ref[...] = (_dot(h_scr[...], wg_ref[...]) + bg_ref[...]).T

    o_ref[...] = _dot(h_scr[...], w_ref[...]).astype(o_ref.dtype)


def _proj_glu_kernel(x_ref, nw_ref, sh_ref, sc_ref, wa_ref, wg_ref, ba_ref, bg_ref, o_ref, h_scr):
    _norm_mod_to_scratch(x_ref, nw_ref, sh_ref, sc_ref, h_scr)
    h = h_scr[...]
    a = _dot(h, wa_ref[...]) + ba_ref[...]
    g = _dot(h, wg_ref[...]) + bg_ref[...]
    o_ref[...] = (a * _sigmoid(g)).astype(o_ref.dtype)


def _rope(y, cos, sins, first_half):
    partner = jnp.where(first_half, pltpu.roll(y, 96, 1), pltpu.roll(y, 32, 1))
    return y * cos + partner * sins


def _proj_gqa_kernel(x_ref, nw_ref, sh_ref, sc_ref, w_ref, qn_ref, kn_ref, cos_ref, sin_ref, o_ref, h_scr):
    _norm_mod_to_scratch(x_ref, nw_ref, sh_ref, sc_ref, h_scr)
    j = pl.program_id(1)
    y = _dot(h_scr[...], w_ref[...])
    tn = y.shape[1]
    n_q = GQA_HEADS * HEAD_DIM // tn
    n_k = GQA_KV_HEADS * HEAD_DIM // tn

    def normed(nw, scale):
        cos = cos_ref[...]
        sins = sin_ref[...]
        lane = lax.broadcasted_iota(jnp.int32, cos.shape, 1)
        first_half = (lane & 63) < 32
        for s in range(tn // HEAD_DIM):
            ys = y[:, s * HEAD_DIM:(s + 1) * HEAD_DIM]
            ys = _rms(ys, nw) * scale
            o_ref[:, s * HEAD_DIM:(s + 1) * HEAD_DIM] = _rope(ys, cos, sins, first_half).astype(o_ref.dtype)

    @pl.when(j < n_q)
    def _():
        normed(qn_ref[...], HEAD_DIM ** -0.5)

    @pl.when(jnp.logical_and(j >= n_q, j < n_q + n_k))
    def _():
        normed(kn_ref[...], 1.0)

    @pl.when(j >= n_q + n_k)
    def _():
        o_ref[...] = y.astype(o_ref.dtype)


def _project(x, nw, sh, sc, w, *, mode, tm, tn, out_dtype=BF16, bias=None, n_scaled=0, scale=1.0,
             extra=None, name):
    rows = x.shape[0]
    n = w.shape[1]
    nt = rows // tm
    mi = _mod_index(tm)
    row_spec = pl.BlockSpec((tm, D_MODEL), lambda i, j: (i, 0))
    vec_spec = pl.BlockSpec((1, D_MODEL), lambda i, j: (0, 0))
    mod_spec = pl.BlockSpec((None, 1, D_MODEL), lambda i, j: (mi(i), 0, 0))
    common = [row_spec, vec_spec, mod_spec, mod_spec]
    args = [x, nw, sh, sc]
    if mode == "plain":
        n_out = n
        if bias is None:
            bias = jnp.zeros((1, n), F32)
        in_specs = common + [pl.BlockSpec((D_MODEL, tn), lambda i, j: (0, j)),
                             pl.BlockSpec((1, tn), lambda i, j: (0, j))]
        args += [w, bias]
        body = functools.partial(_proj_plain_kernel, n_scaled=n_scaled, scale=scale)
    elif mode == "glu":
        n_out = n // 2
        off = n_out // tn
        in_specs = common + [pl.BlockSpec((D_MODEL, tn), lambda i, j: (0, j)),
                             pl.BlockSpec((D_MODEL, tn), lambda i, j: (0, j + off)),
                             pl.BlockSpec((1, tn), lambda i, j: (0, j)),
                             pl.BlockSpec((1, tn), lambda i, j: (0, j + off))]
        args += [w, w, bias, bias]
        body = _proj_glu_kernel
    elif mode == "mlstm":
        n_out = n
        w_g, b_g = extra
        ng = w_g.shape[1]
        in_specs = common + [pl.BlockSpec((D_MODEL, tn), lambda i, j: (0, j)),
                             pl.BlockSpec((D_MODEL, ng), lambda i, j: (0, 0)),
                             pl.BlockSpec((1, ng), lambda i, j: (0, 0))]
        args += [w, w_g, b_g]
        return pl.pallas_call(
            _proj_mlstm_kernel,
            grid=(nt, n_out // tn),
            in_specs=in_specs,
            out_specs=[pl.BlockSpec((tm, tn), lambda i, j: (i, j)),
                       pl.BlockSpec((ng, tm), lambda i, j: (0, i))],
            out_shape=[jax.ShapeDtypeStruct((rows, n_out), out_dtype),
                       jax.ShapeDtypeStruct((ng, rows), F32)],
            scratch_shapes=[pltpu.VMEM((tm, D_MODEL), BF16)],
            compiler_params=_cparams(("parallel", "arbitrary")),
            name=name,
        )(*args)
    else:
        n_out = n
        qn, kn, cos, sins = extra
        head_spec = pl.BlockSpec((1, HEAD_DIM), lambda i, j: (0, 0))
        tab_spec = pl.BlockSpec((tm, HEAD_DIM), lambda i, j: (i, 0))
        in_specs = common + [pl.BlockSpec((D_MODEL, tn), lambda i, j: (0, j)),
                             head_spec, head_spec, tab_spec, tab_spec]
        args += [w, qn, kn, cos, sins]
        body = _proj_gqa_kernel
    return pl.pallas_call(
        body,
        grid=(nt, n_out // tn),
        in_specs=in_specs,
        out_specs=pl.BlockSpec((tm, tn), lambda i, j: (i, j)),
        out_shape=jax.ShapeDtypeStruct((rows, n_out), out_dtype),
        scratch_shapes=[pltpu.VMEM((tm, D_MODEL), BF16)],
        compiler_params=_cparams(("parallel", "arbitrary")),
        name=name,
    )(*args)


def _outproj_kernel(a_ref, w_ref, b_ref, x_ref, pw_ref, g_ref, o_ref):
    y = _dot(a_ref[...], w_ref[...]) + b_ref[...]
    o_ref[...] = x_ref[...] + g_ref[...] * _rms(y, pw_ref[...])


def _outproj_mlstm_kernel(hf_ref, hb_ref, og_ref, on_ref, w_ref, b_ref, x_ref, pw_ref, g_ref, o_ref, a_scr):
    for h in range(MLSTM_HEADS):
        sl = slice(h * MLSTM_DV, (h + 1) * MLSTM_DV)
        hh = hf_ref[:, sl] + hb_ref[:, sl]
        hh = _rms(hh, on_ref[:, sl]) * _sigmoid(og_ref[:, sl].astype(F32))
        a_scr[:, sl] = hh.astype(BF16)
    y = _dot(a_scr[...], w_ref[...]) + b_ref[...]
    o_ref[...] = x_ref[...] + g_ref[...] * _rms(y, pw_ref[...])


def _out_project(a, w, bias, x, pw, g, *, tm, rows_out, name, mlstm=None):
    nt = rows_out // tm
    mi = _mod_index(tm)
    row_spec = pl.BlockSpec((tm, D_MODEL), lambda i: (i, 0))
    vec_spec = pl.BlockSpec((1, D_MODEL), lambda i: (0, 0))
    w_spec = pl.BlockSpec((D_MODEL, D_MODEL), lambda i: (0, 0))
    mod_spec = pl.BlockSpec((None, 1, D_MODEL), lambda i: (mi(i), 0, 0))
    if bias is None:
        bias = jnp.zeros((1, D_MODEL), F32)
    if mlstm is None:
        body = _outproj_kernel
        in_specs = [row_spec, w_spec, vec_spec, row_spec, vec_spec, mod_spec]
        args = [a, w, bias, x, pw, g]
        scratch = []
    else:
        hdir, z, out_norm = mlstm
        body = _outproj_mlstm_kernel
        o_gate_block = (2 * MLSTM_HEADS * MLSTM_DQK + MLSTM_HEADS * MLSTM_DV) // D_MODEL
        in_specs = [pl.BlockSpec((None, tm, D_MODEL), lambda i: (0, i, 0)),
                    pl.BlockSpec((None, tm, D_MODEL), lambda i: (1, i, 0)),
                    pl.BlockSpec((tm, D_MODEL), lambda i: (i, o_gate_block)),
                    vec_spec, w_spec, vec_spec, row_spec, vec_spec, mod_spec]
        args = [hdir, hdir, z, out_norm, w, bias, x, pw, g]
        scratch = [pltpu.VMEM((tm, D_MODEL), BF16)]
    return pl.pallas_call(
        body,
        grid=(nt,),
        in_specs=in_specs,
        out_specs=row_spec,
        out_shape=jax.ShapeDtypeStruct((rows_out, D_MODEL), F32),
        scratch_shapes=scratch,
        compiler_params=_cparams(("parallel",)),
        name=name,
    )(*args)


def _ffn_kernel(x_ref, xp_ref, xn_ref, nw_ref, sh_ref, sc_ref, g_ref, wv_ref, wg_ref, cp_ref,
                wd_ref, pw_ref, o_ref, h_scr, uv_scr, ug_scr, acc_scr, *, tm, tile0):
    i = pl.program_id(0) + tile0
    f = pl.program_id(1)

    @pl.when(f == 0)
    def _():
        def nm(xr):
            return (_rms(xr[...], nw_ref[...]) * (1.0 + sc_ref[...]) + sh_ref[...]).astype(BF16)
        h_scr[0:HALO, :] = nm(xp_ref)
        h_scr[HALO:HALO + tm, :] = nm(x_ref)
        h_scr[HALO + tm:HALO + tm + HALO, :] = nm(xn_ref)
        acc_scr[...] = jnp.zeros_like(acc_scr)

    r = i * tm + lax.broadcasted_iota(jnp.int32, (tm, 1), 0)
    has_left = jnp.where(_is_seq_start(r), 0.0, 1.0)
    has_right = jnp.where(_is_seq_start(r + 1), 0.0, 1.0)

    n_rh, n_sub = uv_scr.shape[0], uv_scr.shape[1]
    hr = tm // n_rh

    def conv(u_scr, p0, cl, rows):
        left = u_scr[HALO - 1:HALO - 1 + hr, :] * has_left[rows]
        right = u_scr[HALO + 1:HALO + 1 + hr, :] * has_right[rows]
        return (left * cp_ref[p0:p0 + 1, cl] + u_scr[HALO:HALO + hr, :] * cp_ref[p0 + 1:p0 + 2, cl]
                + right * cp_ref[p0 + 2:p0 + 3, cl] + cp_ref[p0 + 3:p0 + 4, cl])

    for rh in range(n_rh):
        hh = h_scr[rh * hr:rh * hr + hr + 2 * HALO, :]
        for cb in range(n_sub):
            cl = slice(cb * FFN_SUB, (cb + 1) * FFN_SUB)
            uv_scr[rh, cb] = _dot(hh, wv_ref[:, cl])
            ug_scr[rh, cb] = _dot(hh, wg_ref[:, cl])
    for rh in range(n_rh):
        rows = slice(rh * hr, (rh + 1) * hr)
        acts = []
        for cb in range(n_sub):
            cl = slice(cb * FFN_SUB, (cb + 1) * FFN_SUB)
            val = conv(uv_scr.at[rh, cb], 0, cl, rows)
            gate = conv(ug_scr.at[rh, cb], 4, cl, rows)
            acts.append((val * (gate * _sigmoid(gate))).astype(BF16))
        acc_scr[rows, :] += _dot(jnp.concatenate(acts, axis=1), wd_ref[...])

    @pl.when(f == pl.num_programs(1) - 1)
    def _():
        o_ref[...] = x_ref[...] + g_ref[...] * _rms(acc_scr[...], pw_ref[...])


def _ffn_kernel_into(*refs, n_in, **kw):
    _ffn_kernel(*refs[:n_in], *refs[n_in + 1:], **kw)


def _ffn_conv_params(conv_w, conv_b):
    b = conv_b[:, None, :]
    return jnp.concatenate([conv_w[:, :, :D_FF], b[:, :, :D_FF], conv_w[:, :, D_FF:], b[:, :, D_FF:]], axis=1)


def _conv_ffn(x, nw, sh, sc, g, w_up, conv_p, w_down, pw, *, layer, tm, tf, rows_out, name,
              tile0=0, n_tiles=None, into=None):
    rows = x.shape[0]
    nt = rows_out // tm if n_tiles is None else n_tiles
    nf = D_FF // tf
    mi = _mod_index(tm)
    hb = tm // HALO
    last_halo = rows // HALO - 1
    once = pl.Buffered(1) if tm > 512 else None
    vec_spec = pl.BlockSpec((1, D_MODEL), lambda i, f: (0, 0))
    mod_spec = pl.BlockSpec((None, 1, D_MODEL), lambda i, f: (mi(i + tile0), 0, 0))
    in_specs = [
        pl.BlockSpec((tm, D_MODEL), lambda i, f: (i + tile0, 0), pipeline_mode=once),
        pl.BlockSpec((HALO, D_MODEL), lambda i, f: (jnp.maximum((i + tile0) * hb - 1, 0), 0)),
        pl.BlockSpec((HALO, D_MODEL), lambda i, f: (jnp.minimum((i + tile0 + 1) * hb, last_halo), 0)),
        vec_spec, mod_spec, mod_spec, mod_spec,
        pl.BlockSpec((None, D_MODEL, tf), lambda i, f: (layer, 0, f)),
        pl.BlockSpec((None, D_MODEL, tf), lambda i, f: (layer, 0, f + nf)),
        pl.BlockSpec((None, 8, tf), lambda i, f: (layer, 0, f)),
        pl.BlockSpec((None, tf, D_MODEL), lambda i, f: (layer, f, 0)),
        vec_spec,
    ]
    args = [x, x, x, nw, sh, sc, g, w_up, w_up, conv_p, w_down, pw]
    body = functools.partial(_ffn_kernel, tm=tm, tile0=tile0)
    aliases = {}
    if into is not None:
        body = functools.partial(_ffn_kernel_into, n_in=len(args), tm=tm, tile0=tile0)
        aliases = {len(args): 0}
        in_specs = in_specs + [pl.BlockSpec(memory_space=pl.ANY)]
        args = args + [into]
    return pl.pallas_call(
        body,
        grid=(nt, nf),
        in_specs=in_specs,
        out_specs=pl.BlockSpec((tm, D_MODEL), lambda i, f: (i + tile0, 0), pipeline_mode=once),
        out_shape=jax.ShapeDtypeStruct((rows_out, D_MODEL), F32),
        input_output_aliases=aliases,
        scratch_shapes=[pltpu.VMEM((tm + 2 * HALO, D_MODEL), BF16),
                        pltpu.VMEM((FFN_ROW_BLOCKS, tf // FFN_SUB, tm // FFN_ROW_BLOCKS + 2 * HALO, FFN_SUB), F32),
                        pltpu.VMEM((FFN_ROW_BLOCKS, tf // FFN_SUB, tm // FFN_ROW_BLOCKS + 2 * HALO, FFN_SUB), F32),
                        pltpu.VMEM((tm, D_MODEL), F32)],
        compiler_params=_cparams(("parallel", "arbitrary")),
        name=name,
    )(*args)


def _exp_bf16(x):
    return jnp.exp(x.astype(BF16))


def _with_ones(v):
    return jnp.concatenate([v, jnp.ones_like(v)], axis=1)


GQA_SCORES_AHEAD = 3


def _gqa_attn_kernel(q_ref, kl_ref, vl_ref, kc_ref, vc_ref, o_ref, vla_scr, vca_scr):
    t = pl.program_id(2)
    n_lat_tiles = pl.num_programs(2) - 1
    group = GQA_HEADS // GQA_KV_HEADS
    kc = kc_ref[...]

    @pl.when(t == 0)
    def _():
        vla_scr[...] = _with_ones(vl_ref[...])
        vca_scr[...] = _with_ones(vc_ref[...])

    @pl.when(t < n_lat_tiles)
    def _():
        kl = kl_ref[...]

        def scores(gi):
            q = q_ref[:, gi * HEAD_DIM:(gi + 1) * HEAD_DIM]
            return _dot_nt(q, kc), _dot_nt(q, kl)

        ahead = GQA_SCORES_AHEAD
        pending = [scores(gi) for gi in range(min(ahead, group))]
        for gi in range(group):
            sl = slice(gi * HEAD_DIM, (gi + 1) * HEAD_DIM)
            s_c, s_l = pending.pop(0)
            if gi + ahead < group:
                pending.append(scores(gi + ahead))
            m = jnp.maximum(jnp.max(s_c, axis=1, keepdims=True), jnp.max(s_l, axis=1, keepdims=True))
            p_c = _exp_bf16(s_c - m)
            p_l = _exp_bf16(s_l - m)
            hq = p_l.shape[0] // 2
            for rs in (slice(0, hq), slice(hq, 2 * hq)):
                o = _dot(p_c[rs], vca_scr[...]) + _dot(p_l[rs], vla_scr[...])
                o_ref[rs, sl] = (o[:, :HEAD_DIM] / o[:, HEAD_DIM:HEAD_DIM + 1]).astype(o_ref.dtype)

    @pl.when(t == n_lat_tiles)
    def _():
        for gi in range(group):
            sl = slice(gi * HEAD_DIM, (gi + 1) * HEAD_DIM)
            q = q_ref[:, sl]
            s_c = _dot_nt(q, kc)
            p_c = _exp_bf16(s_c - jnp.max(s_c, axis=1, keepdims=True))
            o = _dot(p_c, vca_scr[...])
            o_ref[:, sl] = (o[:, :HEAD_DIM] / o[:, HEAD_DIM:HEAD_DIM + 1]).astype(o_ref.dtype)


def _gqa_attention(z):
    tq = SEG
    n_lat = SEQ // tq
    group_w = (GQA_HEADS // GQA_KV_HEADS) * HEAD_DIM
    k_col = GQA_HEADS
    v_col = GQA_HEADS + GQA_KV_HEADS
    ctx_blk = R_LAT // SEG

    def q_rows(b, kh, t):
        return jnp.where(t < n_lat, b * n_lat + t, ctx_blk + b)

    return pl.pallas_call(
        _gqa_attn_kernel,
        grid=(BATCH, GQA_KV_HEADS, n_lat + 1),
        in_specs=[
            pl.BlockSpec((tq, group_w), lambda b, kh, t: (q_rows(b, kh, t), kh)),
            pl.BlockSpec((SEQ, HEAD_DIM), lambda b, kh, t: (b, k_col + kh)),
            pl.BlockSpec((SEQ, HEAD_DIM), lambda b, kh, t: (b, v_col + kh)),
            pl.BlockSpec((CTX_LEN, HEAD_DIM), lambda b, kh, t: (ctx_blk + b, k_col + kh)),
            pl.BlockSpec((CTX_LEN, HEAD_DIM), lambda b, kh, t: (ctx_blk + b, v_col + kh)),
        ],
        out_specs=pl.BlockSpec((tq, group_w), lambda b, kh, t: (q_rows(b, kh, t), kh)),
        out_shape=jax.ShapeDtypeStruct((R_ALL, GQA_HEADS * HEAD_DIM), BF16),
        scratch_shapes=[pltpu.VMEM((SEQ, 2 * HEAD_DIM), BF16),
                        pltpu.VMEM((CTX_LEN, 2 * HEAD_DIM), BF16)],
        compiler_params=_cparams(("parallel", "parallel", "arbitrary")),
        name="gqa_attention",
    )(z, z, z, z, z)


def _split3(x):
    hi = x.astype(BF16)
    r1 = x - hi.astype(F32)
    mid = r1.astype(BF16)
    lo = (r1 - mid.astype(F32)).astype(BF16)
    return hi, mid, lo


def _log_sigmoid(x):
    return jnp.minimum(x, 0.0) - jnp.log(1.0 + jnp.exp(-jnp.abs(x)))


MLSTM_AUG = MLSTM_DV + 128


def _mlstm_kernel(q_ref, k_ref, v_ref, g_ref, o_ref, c_scr, m_scr):
    d = pl.program_id(1)
    c = pl.program_id(2)
    nh = MLSTM_HEADS
    L = SEG
    scale = MLSTM_DQK ** -0.5

    @pl.when(c == 0)
    def _():
        c_scr[...] = jnp.zeros_like(c_scr)
        m_scr[...] = jnp.full_like(m_scr, M_INIT)

    ri = lax.broadcasted_iota(jnp.int32, (L, L), 0)
    ci = lax.broadcasted_iota(jnp.int32, (L, L), 1)
    diff = jnp.where(d == 0, ci - ri, ri - ci)
    sees = diff <= 0
    cum_t = jnp.where(diff >= 0, 1.0, 0.0).astype(BF16)

    g = g_ref[...]
    ig = g[0:nh]
    lf = _log_sigmoid(g[nh:2 * nh])
    b = sum(_dot(p, cum_t) for p in _split3(lf))
    g_tot = jnp.sum(lf, axis=1, keepdims=True)
    m_in = m_scr[:, 0:1]
    cr = ig - b
    lane = lax.broadcasted_iota(jnp.int32, (nh, L), 1)
    pm = cr
    for step in range(L.bit_length() - 1):
        sh = 1 << step
        prev = jnp.where(lane >= sh, pltpu.roll(pm, sh, 1), NEG)
        nxt = jnp.where(lane < L - sh, pltpu.roll(pm, L - sh, 1), NEG)
        pm = jnp.maximum(pm, jnp.where(d == 0, prev, nxt))
    big_m = jnp.maximum(m_in, pm)
    w_inter = jnp.exp(m_in - big_m) * scale
    floor = jnp.exp(-(b + big_m))
    a = g_tot - b + ig
    m_loc = jnp.max(a, axis=1, keepdims=True)
    w = jnp.exp(a - m_loc)
    m_new = jnp.maximum(g_tot + m_in, m_loc)
    a_old = jnp.exp(g_tot + m_in - m_new)
    a_new = jnp.exp(m_loc - m_new)
    slab = jnp.concatenate([big_m, w_inter, floor, jnp.zeros((128 - 3 * nh, L), F32)], axis=0)
    cols = slab.T

    ones = jnp.ones((L, MLSTM_AUG - MLSTM_DV), BF16)
    for h in range(nh):
        q = q_ref[:, h * MLSTM_DQK:(h + 1) * MLSTM_DQK]
        k = k_ref[:, h * MLSTM_DQK:(h + 1) * MLSTM_DQK]
        v_aug = jnp.concatenate([v_ref[:, h * MLSTM_DV:(h + 1) * MLSTM_DV], ones], axis=1)
        e = jnp.where(sees, cr[h:h + 1, :] - cols[:, h:h + 1], NEG)
        s = _dot_nt(q, k) * (jnp.exp(e) * scale)
        c_in = c_scr[h]
        num = _dot(s.astype(BF16), v_aug) + cols[:, nh + h:nh + h + 1] * _dot(q, c_in.astype(BF16))
        den = jnp.maximum(jnp.abs(num[:, MLSTM_DV:MLSTM_DV + 1]), cols[:, 2 * nh + h:2 * nh + h + 1])
        o_ref[:, h * MLSTM_DV:(h + 1) * MLSTM_DV] = num[:, :MLSTM_DV] / den

        kw_t = (k.astype(F32).T * w[h:h + 1, :]).astype(BF16)
        c_scr[h] = a_old[h:h + 1, :] * c_in + a_new[h:h + 1, :] * _dot(kw_t, v_aug)
    m_scr[...] = jnp.broadcast_to(m_new, m_scr.shape)


def _mlstm(z, gates_t):
    n_lat = SEQ // SEG
    ctx_blk = R_LAT // SEG
    nqk = MLSTM_HEADS * MLSTM_DQK
    nv = MLSTM_HEADS * MLSTM_DV

    def rb(b, d, c):
        lat = jnp.where(d == 0, c - 1, n_lat - c)
        return jnp.where(c == 0, ctx_blk + b, b * n_lat + lat)

    return pl.pallas_call(
        _mlstm_kernel,
        grid=(BATCH, 2, n_lat + 1),
        in_specs=[
            pl.BlockSpec((SEG, nqk), lambda b, d, c: (rb(b, d, c), 0)),
            pl.BlockSpec((SEG, nqk), lambda b, d, c: (rb(b, d, c), 1)),
            pl.BlockSpec((SEG, nv), lambda b, d, c: (rb(b, d, c), 2 * nqk // nv)),
            pl.BlockSpec((2 * MLSTM_HEADS, SEG), lambda b, d, c: (d, rb(b, d, c))),
        ],
        out_specs=pl.BlockSpec((None, SEG, nv), lambda b, d, c: (d, rb(b, d, c), 0)),
        out_shape=jax.ShapeDtypeStruct((2, R_ALL, nv), F32),
        scratch_shapes=[pltpu.VMEM((MLSTM_HEADS, MLSTM_DQK, MLSTM_AUG), F32),
                        pltpu.VMEM((MLSTM_HEADS, 128), F32)],
        compiler_params=_cparams(("parallel", "parallel", "arbitrary")),
        name="mlstm",
    )(z, z, z, gates_t)


DW_ROWS = 32
DW_COLS = 512


def _dwconv_ln_kernel(u_ref, up_ref, un_ref, w_ref, b_ref, lw_ref, lb_ref, o_ref, pad_scr, conv_scr, sh_scr):
    i = pl.program_id(0)
    tm = u_ref.shape[0]
    r0 = i * tm
    keep_prev = jnp.logical_not(_is_seq_start(r0))
    keep_next = jnp.logical_not(_is_seq_start(r0 + tm))
    pad_scr[0:HALO, :] = jnp.where(keep_prev, up_ref[...].astype(F32), 0.0)
    pad_scr[HALO:HALO + tm, :] = u_ref[...].astype(F32)
    pad_scr[HALO + tm:HALO + tm + HALO, :] = jnp.where(keep_next, un_ref[...].astype(F32), 0.0)

    rc = DW_ROWS
    cw = DW_COLS
    half = CONV_WIDTH // 2
    n_sh = sh_scr.shape[1]

    def chunk(cs, carry):
        cl = pl.ds(pl.multiple_of(cs * cw, cw), cw)
        for s in range(1, 8):
            sh_scr[s] = pad_scr[s:s + n_sh, cl]
        for ib in range(tm // rc):
            base = ib * rc
            acc = jnp.zeros((rc, cw), F32) + b_ref[:, cl]
            for kk in range(CONV_WIDTH):
                al, s = divmod(HALO - half + kk, 8)
                off = base + 8 * al
                tap = pad_scr[off:off + rc, cl] if s == 0 else sh_scr[s, off:off + rc, :]
                acc = acc + tap * w_ref[kk:kk + 1, cl]
            conv_scr[base:base + rc, cl] = acc
        return carry

    lax.fori_loop(0, D_MODEL // cw, chunk, 0)
    y = conv_scr[...]
    mu = jnp.mean(y, axis=-1, keepdims=True)
    yc = y - mu
    var = jnp.mean(yc * yc, axis=-1, keepdims=True)
    yn = yc * lax.rsqrt(var + EPS) * lw_ref[...] + lb_ref[...]
    o_ref[...] = (yn * _sigmoid(yn)).astype(o_ref.dtype)


def _dwconv_ln_swish(u, w_dw, b_dw, ln_w, ln_b):
    rows = u.shape[0]
    tm = SEG
    hb = tm // HALO
    last_halo = rows // HALO - 1
    vec_spec = pl.BlockSpec((1, D_MODEL), lambda i: (0, 0))
    return pl.pallas_call(
        _dwconv_ln_kernel,
        grid=(rows // tm,),
        in_specs=[
            pl.BlockSpec((tm, D_MODEL), lambda i: (i, 0)),
            pl.BlockSpec((HALO, D_MODEL), lambda i: (jnp.maximum(i * hb - 1, 0), 0)),
            pl.BlockSpec((HALO, D_MODEL), lambda i: (jnp.minimum((i + 1) * hb, last_halo), 0)),
            pl.BlockSpec((CONV_WIDTH, D_MODEL), lambda i: (0, 0)),
            vec_spec, vec_spec, vec_spec,
        ],
        out_specs=pl.BlockSpec((tm, D_MODEL), lambda i: (i, 0)),
        out_shape=jax.ShapeDtypeStruct((rows, D_MODEL), BF16),
        scratch_shapes=[pltpu.VMEM((tm + 2 * HALO, D_MODEL), F32),
                        pltpu.VMEM((tm, D_MODEL), F32),
                        pltpu.VMEM((8, tm + 2 * HALO - 8, DW_COLS), F32)],
        compiler_params=_cparams(("parallel",)),
        name="dwconv_ln_swish",
    )(u, u, u, w_dw, b_dw, ln_w, ln_b)


NAT_Q_ROWS = 4
NAT_BAND_ROWS = 12
NAT_TQ = NAT_Q_ROWS * GRID_W
NAT_BAND = NAT_BAND_ROWS * GRID_W


NAT_HEADS_PER_STEP = 4


def _nat_kernel(q_ref, kl_ref, vl_ref, kc_ref, vc_ref, bias_ref, o_ref, vla_scr, vca_scr):
    gq = pl.program_id(2)
    max_start = (SEQ - NAT_BAND) // NAT_TQ
    start = pl.multiple_of(jnp.clip(gq - 1, 0, max_start) * NAT_TQ, NAT_TQ)
    hd = NA_HEAD_DIM

    @pl.when(gq == 0)
    def _():
        for hh in range(NAT_HEADS_PER_STEP):
            sl = slice(hh * hd, (hh + 1) * hd)
            vla_scr[:, 2 * hh * hd:2 * (hh + 1) * hd] = _with_ones(vl_ref[:, sl])
            vca_scr[:, 2 * hh * hd:2 * (hh + 1) * hd] = _with_ones(vc_ref[:, sl])

    def scores(hh):
        sl = slice(hh * hd, (hh + 1) * hd)
        q = q_ref[:, sl]
        return _dot_nt(q, kl_ref[pl.ds(start, NAT_BAND), sl]) + bias_ref[hh], _dot_nt(q, kc_ref[:, sl])

    s_next = scores(0)
    for hh in range(NAT_HEADS_PER_STEP):
        sl = slice(hh * hd, (hh + 1) * hd)
        sl2 = slice(2 * hh * hd, 2 * (hh + 1) * hd)
        s_n, s_c = s_next
        if hh + 1 < NAT_HEADS_PER_STEP:
            s_next = scores(hh + 1)
        m = jnp.maximum(jnp.max(s_n, axis=1, keepdims=True), jnp.max(s_c, axis=1, keepdims=True))
        p_n = _exp_bf16(s_n - m)
        p_c = _exp_bf16(s_c - m)
        o = _dot(p_n, vla_scr[pl.ds(start, NAT_BAND), sl2]) + _dot(p_c, vca_scr[:, sl2])
        o_ref[:, sl] = (o[:, :hd] / o[:, hd:hd + 1]).astype(o_ref.dtype)


def _nat_bias_blocks():
    rows = SEQ // GRID_W
    n_groups = rows // NAT_Q_ROWS
    q_col = np.arange(GRID_W)
    c_start = np.clip(q_col - NA_COLS // 2, 0, GRID_W - NA_COLS)
    dc = q_col[None, :] - c_start[:, None]
    col_ok = (dc >= 0) & (dc < NA_COLS)
    masked = 2 * NA_ROWS - 1
    block = np.full((3, NAT_Q_ROWS, NAT_BAND_ROWS), masked, np.int64)
    for var, gq in enumerate((0, 1, n_groups - 1)):
        lo = int(np.clip(gq - 1, 0, (SEQ - NAT_BAND) // NAT_TQ)) * NAT_Q_ROWS
        for qr in range(NAT_Q_ROWS):
            q_row = gq * NAT_Q_ROWS + qr
            r_start = int(np.clip(q_row - NA_ROWS // 2, 0, rows - NA_ROWS))
            for kr in range(NAT_BAND_ROWS):
                k_row = lo + kr
                if 0 <= k_row - r_start < NA_ROWS:
                    block[var, qr, kr] = k_row - q_row + (NA_ROWS - 1)
    return col_ok, block


def _nat_bias(rpb):
    n_r, n_c = 2 * NA_ROWS - 1, 2 * NA_COLS - 1
    w = GRID_W
    col_ok, block = _nat_bias_blocks()
    lead = w - NA_COLS
    vext = jnp.pad(rpb, ((0, 0), (0, 0), (lead, 2 * w - lead - n_c)))
    skew = jnp.broadcast_to(vext[:, :, None, :], (NA_HEADS, n_r, w, 2 * w)).reshape(NA_HEADS, n_r, 2 * w * w)
    skew = skew[:, :, :w * (2 * w - 1)].reshape(NA_HEADS, n_r, w, 2 * w - 1)
    tb = jnp.where(col_ok[None, None], skew[..., w - 1:], NEG)
    tb = jnp.concatenate([tb, jnp.full((NA_HEADS, 1, w, w), NEG, F32)], axis=1)
    strips = [jnp.concatenate([tb[:, int(d)] for d in block[var, qr]], axis=-1)
              for var in range(3) for qr in range(NAT_Q_ROWS)]
    return jnp.stack(strips, axis=1).reshape(NA_HEADS, 3, NAT_TQ, NAT_BAND)


def _nat_attention(z, rpb):
    n_groups = SEQ // NAT_TQ
    ctx_blk = R_LAT // SEG
    bias = _nat_bias(rpb)
    hps = NAT_HEADS_PER_STEP
    gw = hps * NA_HEAD_DIM

    def variant(gq):
        return (gq > 0).astype(jnp.int32) + (gq == n_groups - 1).astype(jnp.int32)

    k_col = NA_HEADS // hps
    v_col = 2 * NA_HEADS // hps
    return pl.pallas_call(
        _nat_kernel,
        grid=(BATCH, NA_HEADS // hps, n_groups),
        in_specs=[
            pl.BlockSpec((NAT_TQ, gw), lambda b, h, gq: (b * n_groups + gq, h)),
            pl.BlockSpec((SEQ, gw), lambda b, h, gq: (b, k_col + h)),
            pl.BlockSpec((SEQ, gw), lambda b, h, gq: (b, v_col + h)),
            pl.BlockSpec((CTX_LEN, gw), lambda b, h, gq: (ctx_blk + b, k_col + h)),
            pl.BlockSpec((CTX_LEN, gw), lambda b, h, gq: (ctx_blk + b, v_col + h)),
            pl.BlockSpec((hps, None, NAT_TQ, NAT_BAND), lambda b, h, gq: (h, variant(gq), 0, 0)),
        ],
        out_specs=pl.BlockSpec((NAT_TQ, gw), lambda b, h, gq: (b * n_groups + gq, h)),
        out_shape=jax.ShapeDtypeStruct((R_LAT, NA_HEADS * NA_HEAD_DIM), BF16),
        scratch_shapes=[pltpu.VMEM((SEQ, 2 * gw), BF16),
                        pltpu.VMEM((CTX_LEN, 2 * gw), BF16)],
        compiler_params=_cparams(("parallel", "parallel", "arbitrary")),
        name="nat_attention",
    )(z, z, z, z, z, bias)


def _rope_tables():
    pos = jnp.arange(SEQ)
    quarter = HEAD_DIM // 4
    freqs = ROPE_THETA ** (-jnp.arange(quarter, dtype=F32) / quarter)

    def cs(p):
        ang = p.astype(F32)[:, None] * freqs[None, :]
        return jnp.cos(ang), jnp.sin(ang)

    cr, sr = cs(pos // GRID_W)
    cc, sc = cs(pos % GRID_W)
    cos = jnp.concatenate([cr, cr, cc, cc], axis=1)
    sins = jnp.concatenate([-sr, sr, -sc, sc], axis=1)
    n_ctx = BATCH * CTX_LEN
    cos = jnp.concatenate([cos] * BATCH + [jnp.ones((n_ctx, HEAD_DIM), F32)], axis=0)
    sins = jnp.concatenate([sins] * BATCH + [jnp.zeros((n_ctx, HEAD_DIM), F32)], axis=0)
    return cos, sins


def kernel(x, c, ctx, c_ctx, mod_w, mod_b, norm_pre_mix, norm_post_mix, norm_pre_ffn, norm_post_ffn, ffn_w_up, ffn_conv_w, ffn_conv_b, ffn_w_down, gqa_w_in, gqa_q_norm, gqa_k_norm, gqa_w_out, mlstm_w_in, mlstm_b_gate, mlstm_out_norm, mlstm_w_out, conv_w_pw1, conv_b_pw1, conv_w_dw, conv_b_dw, conv_ln_w, conv_ln_b, conv_w_pw2, conv_b_pw2, nat_w_in, nat_rpb, nat_w_out):
    tm = 512
    xs = jnp.concatenate([x.reshape(R_LAT, D_MODEL), ctx.reshape(BATCH * CTX_LEN, D_MODEL)], axis=0)
    a8 = jnp.concatenate([c, c_ctx[None, :], jnp.zeros((8 - BATCH - 1, D_MODEL), F32)], axis=0)
    mods = _modulation(a8, mod_w, mod_b)
    mods = mods[:, :BATCH + 1].reshape(DEPTH, BATCH + 1, MOD_CHUNKS, 1, D_MODEL)
    cos, sins = _rope_tables()
    w_up_bf16 = ffn_w_up.astype(BF16)
    w_down_bf16 = ffn_w_down.astype(BF16)
    conv_p = _ffn_conv_params(ffn_conv_w, ffn_conv_b)

    def row(v):
        return v.reshape(1, -1)

    for i in range(DEPTH):
        kind = i % 4
        last = i == DEPTH - 1
        rows_out = R_LAT if last else R_ALL
        sh1, sc1, g1, sh2, sc2, g2 = [mods[i, :, m] for m in range(MOD_CHUNKS)]
        pre = row(norm_pre_mix[i])
        post = row(norm_post_mix[i])
        x_dead = xs
        if kind == 0:
            z = _project(xs, pre, sh1, sc1, gqa_w_in[0].astype(BF16), mode="gqa", tm=tm, tn=512,
                         extra=(row(gqa_q_norm[0]), row(gqa_k_norm[0]), cos, sins), name="gqa_in")
            a = _gqa_attention(z)
            xs = _out_project(a, gqa_w_out[0].astype(BF16), None, xs, post, g1, tm=tm, rows_out=rows_out,
                              name="gqa_out")
        elif kind == 1:
            n_main = 2 * MLSTM_HEADS * MLSTM_DQK + 2 * MLSTM_HEADS * MLSTM_DV
            n_gate = 4 * MLSTM_HEADS
            w_in = mlstm_w_in[0]
            w_g = jnp.pad(w_in[:, n_main:], ((0, 0), (0, 128 - n_gate))).astype(BF16)
            b_g = jnp.pad(mlstm_b_gate[0], (0, 128 - n_gate)).reshape(1, 128)
            z, gates_t = _project(xs, pre, sh1, sc1, w_in[:, :n_main].astype(BF16), mode="mlstm", tm=tm, tn=2048,
                                  extra=(w_g, b_g), name="mlstm_in")
            hdir = _mlstm(z, gates_t)
            xs = _out_project(None, mlstm_w_out[0].astype(BF16), None, xs, post, g1, tm=SEG, rows_out=rows_out,
                              name="mlstm_out", mlstm=(hdir, z, row(mlstm_out_norm[0])))
        elif kind == 2:
            u = _project(xs, pre, sh1, sc1, conv_w_pw1[0].astype(BF16), mode="glu", tm=tm, tn=1024,
                         bias=row(conv_b_pw1[0]), name="conformer_pw1")
            a = _dwconv_ln_swish(u, conv_w_dw[0], row(conv_b_dw[0]), row(conv_ln_w[0]), row(conv_ln_b[0]))
            xs = _out_project(a, conv_w_pw2[0].astype(BF16), row(conv_b_pw2[0]), xs, post, g1, tm=tm,
                              rows_out=rows_out, name="conformer_pw2")
        else:
            nq = NA_HEADS * NA_HEAD_DIM
            z = _project(xs, pre, sh1, sc1, nat_w_in[0].astype(BF16), mode="plain", tm=tm, tn=2048,
                         n_scaled=nq // 2048, scale=NA_HEAD_DIM ** -0.5, name="nat_in")
            a = _nat_attention(z, nat_rpb[0])
            xs = _out_project(a, nat_w_out[0].astype(BF16), None, xs, post, g1, tm=tm, rows_out=rows_out,
                              name="nat_out")
        ffn_args = (xs, row(norm_pre_ffn[i]), sh2, sc2, g2, w_up_bf16, conv_p, w_down_bf16, row(norm_post_ffn[i]))
        x_new = _conv_ffn(*ffn_args, layer=i, tm=FFN_TM_LATENT, tf=FFN_TF, rows_out=rows_out,
                          n_tiles=R_LAT // FFN_TM_LATENT, into=None if last else x_dead, name="conv_ffn")
        if not last:
            x_new = _conv_ffn(*ffn_args, layer=i, tm=tm, tf=FFN_TF, rows_out=rows_out, tile0=R_LAT // tm,
                              n_tiles=(R_ALL - R_LAT) // tm, into=x_new, name="conv_ffn_ctx")
        xs = x_new
    return xs.reshape(BATCH, SEQ, D_MODEL)
```

```python
import functools

import numpy as np
import jax
import jax.numpy as jnp
from jax import lax
from jax.experimental import pallas as pl
from jax.experimental.pallas import tpu as pltpu

F32 = jnp.float32
BF16 = jnp.bfloat16

D_MODEL = 2048
BATCH = 2
SEQ = 4096
DEPTH = 4
GRID_W = 64
CTX_LEN = 256
EPS = 1e-6
MOD_CHUNKS = 6
D_FF = 5632
GQA_HEADS = 16
GQA_KV_HEADS = 4
HEAD_DIM = 128
ROPE_THETA = 10000.0
MLSTM_HEADS = 8
MLSTM_DQK = 128
MLSTM_DV = 256
M_INIT = -1e30
CONV_WIDTH = 31
NA_HEADS = 16
NA_HEAD_DIM = 128
NA_ROWS = 8
NA_COLS = 16

R_LAT = BATCH * SEQ
R_ALL = R_LAT + BATCH * CTX_LEN
NEG = -1e30

VMEM_LIMIT = 56 * 1024 * 1024
SEG = 256
HALO = 16
FFN_SUB = 256
FFN_ROW_BLOCKS = 2
FFN_TF = 512
FFN_TM_LATENT = 1024


def _cparams(sem, flags=None):
    return pltpu.CompilerParams(dimension_semantics=sem, vmem_limit_bytes=VMEM_LIMIT, flags=flags)


def _sigmoid(x):
    return 1.0 / (1.0 + jnp.exp(-x))


def _rms(x, w):
    return x * lax.rsqrt(jnp.mean(x * x, axis=-1, keepdims=True) + EPS) * w


def _dot(a, b):
    return jnp.dot(a, b, preferred_element_type=F32)


def _dot_nt(a, b):
    return lax.dot_general(a, b, (((1,), (1,)), ((), ())), preferred_element_type=F32)


def _dot_tn(a, b):
    return lax.dot_general(a, b, (((0,), (0,)), ((), ())), preferred_element_type=F32)


def _is_seq_start(r):
    return (r & jnp.where(r < R_LAT, SEQ - 1, CTX_LEN - 1)) == 0


def _mod_index(tm):
    per_batch = SEQ // tm
    return lambda i: jnp.minimum(i // per_batch, BATCH)


def _mod_kernel(a_ref, w_ref, b_ref, o_ref):
    a = a_ref[...]
    s = (a * _sigmoid(a)).astype(BF16)
    o_ref[...] = _dot(s, w_ref[...].astype(BF16)) + b_ref[...]


def _modulation(a8, mod_w, mod_b):
    tn = 1024
    n = MOD_CHUNKS * D_MODEL
    return pl.pallas_call(
        _mod_kernel,
        grid=(DEPTH, n // tn),
        in_specs=[
            pl.BlockSpec((8, D_MODEL), lambda l, j: (0, 0)),
            pl.BlockSpec((None, D_MODEL, tn), lambda l, j: (l, 0, j)),
            pl.BlockSpec((None, 1, tn), lambda l, j: (l, 0, j)),
        ],
        out_specs=pl.BlockSpec((None, 8, tn), lambda l, j: (l, 0, j)),
        out_shape=jax.ShapeDtypeStruct((DEPTH, 8, n), F32),
        compiler_params=_cparams(("parallel", "parallel")),
        name="modulation",
    )(a8, mod_w, mod_b.reshape(DEPTH, 1, n))


def _norm_mod_to_scratch(x_ref, nw_ref, sh_ref, sc_ref, h_scr):
    @pl.when(pl.program_id(1) == 0)
    def _():
        h = _rms(x_ref[...], nw_ref[...]) * (1.0 + sc_ref[...]) + sh_ref[...]
        h_scr[...] = h.astype(BF16)


def _proj_plain_kernel(x_ref, nw_ref, sh_ref, sc_ref, w_ref, b_ref, o_ref, h_scr, *, n_scaled, scale):
    _norm_mod_to_scratch(x_ref, nw_ref, sh_ref, sc_ref, h_scr)
    y = _dot(h_scr[...], w_ref[...]) + b_ref[...]
    if n_scaled:
        y = y * jnp.where(pl.program_id(1) < n_scaled, scale, 1.0).astype(F32)
    o_ref[...] = y.astype(o_ref.dtype)


def _proj_mlstm_kernel(x_ref, nw_ref, sh_ref, sc_ref, w_ref, wg_ref, bg_ref, o_ref, gt_ref, h_scr):
    _norm_mod_to_scratch(x_ref, nw_ref, sh_ref, sc_ref, h_scr)

    @pl.when(pl.program_id(1) == 0)
    def _():
        gt_ref[...] = (_dot(h_scr[...], wg_ref[...]) + bg_ref[...]).T

    o_ref[...] = _dot(h_scr[...], w_ref[...]).astype(o_ref.dtype)


def _proj_glu_kernel(x_ref, nw_ref, sh_ref, sc_ref, wa_ref, wg_ref, ba_ref, bg_ref, o_ref, h_scr):
    _norm_mod_to_scratch(x_ref, nw_ref, sh_ref, sc_ref, h_scr)
    h = h_scr[...]
    a = _dot(h, wa_ref[...]) + ba_ref[...]
    g = _dot(h, wg_ref[...]) + bg_ref[...]
    o_ref[...] = (a * _sigmoid(g)).astype(o_ref.dtype)


def _rope(y, cos, sins, first_half):
    partner = jnp.where(first_half, pltpu.roll(y, 96, 1), pltpu.roll(y, 32, 1))
    return y * cos + partner * sins


def _proj_gqa_kernel(x_ref, nw_ref, sh_ref, sc_ref, w_ref, qn_ref, kn_ref, cos_ref, sin_ref, o_ref, h_scr):
    _norm_mod_to_scratch(x_ref, nw_ref, sh_ref, sc_ref, h_scr)
    j = pl.program_id(1)
    tn = o_ref.shape[1]
    nq = GQA_HEADS * HEAD_DIM
    nkv = GQA_KV_HEADS * HEAD_DIM
    sub = 2 * HEAD_DIM
    cos = cos_ref[...]
    sins = sin_ref[...]
    lane = lax.broadcasted_iota(jnp.int32, cos.shape, 1)
    first_half = (lane & 63) < 32
    h = h_scr[...]
    for cb in range(tn // sub):
        col0 = j * tn + cb * sub
        is_q = col0 < nq
        is_v = col0 >= nq + nkv
        nw = jnp.where(is_q, qn_ref[...], kn_ref[...])
        scale = jnp.where(is_q, HEAD_DIM ** -0.5, 1.0).astype(F32)
        y = _dot(h, w_ref[:, cb * sub:(cb + 1) * sub])
        for s in range(sub // HEAD_DIM):
            ys = y[:, s * HEAD_DIM:(s + 1) * HEAD_DIM]
            rot = _rope(_rms(ys, nw) * scale, cos, sins, first_half)
            c0 = cb * sub + s * HEAD_DIM
            o_ref[:, c0:c0 + HEAD_DIM] = jnp.where(is_v, ys, rot).astype(o_ref.dtype)


def _project(x, nw, sh, sc, w, *, mode, tm, tn, out_dtype=BF16, bias=None, n_scaled=0, scale=1.0,
             extra=None, name):
    rows = x.shape[0]
    n = w.shape[1]
    nt = rows // tm
    mi = _mod_index(tm)
    row_spec = pl.BlockSpec((tm, D_MODEL), lambda i, j: (i, 0))
    vec_spec = pl.BlockSpec((1, D_MODEL), lambda i, j: (0, 0))
    mod_spec = pl.BlockSpec((None, 1, D_MODEL), lambda i, j: (mi(i), 0, 0))
    common = [row_spec, vec_spec, mod_spec, mod_spec]
    args = [x, nw, sh, sc]
    if mode == "plain":
        n_out = n
        if bias is None:
            bias = jnp.zeros((1, n), F32)
        in_specs = common + [pl.BlockSpec((D_MODEL, tn), lambda i, j: (0, j)),
                             pl.BlockSpec((1, tn), lambda i, j: (0, j))]
        args += [w, bias]
        body = functools.partial(_proj_plain_kernel, n_scaled=n_scaled, scale=scale)
    elif mode == "glu":
        n_out = n // 2
        off = n_out // tn
        in_specs = common + [pl.BlockSpec((D_MODEL, tn), lambda i, j: (0, j)),
                             pl.BlockSpec((D_MODEL, tn), lambda i, j: (0, j + off)),
                             pl.BlockSpec((1, tn), lambda i, j: (0, j)),
                             pl.BlockSpec((1, tn), lambda i, j: (0, j + off))]
        args += [w, w, bias, bias]
        body = _proj_glu_kernel
    elif mode == "mlstm":
        n_out = n
        w_g, b_g = extra
        ng = w_g.shape[1]
        in_specs = common + [pl.BlockSpec((D_MODEL, tn), lambda i, j: (0, j)),
                             pl.BlockSpec((D_MODEL, ng), lambda i, j: (0, 0)),
                             pl.BlockSpec((1, ng), lambda i, j: (0, 0))]
        args += [w, w_g, b_g]
        return pl.pallas_call(
            _proj_mlstm_kernel,
            grid=(nt, n_out // tn),
            in_specs=in_specs,
            out_specs=[pl.BlockSpec((tm, tn), lambda i, j: (i, j)),
                       pl.BlockSpec((ng, tm), lambda i, j: (0, i))],
            out_shape=[jax.ShapeDtypeStruct((rows, n_out), out_dtype),
                       jax.ShapeDtypeStruct((ng, rows), F32)],
            scratch_shapes=[pltpu.VMEM((tm, D_MODEL), BF16)],
            compiler_params=_cparams(("parallel", "arbitrary")),
            name=name,
        )(*args)
    else:
        n_out = n
        qn, kn, cos, sins = extra
        head_spec = pl.BlockSpec((1, HEAD_DIM), lambda i, j: (0, 0))
        tab_spec = pl.BlockSpec((tm, HEAD_DIM), lambda i, j: (i, 0))
        in_specs = common + [pl.BlockSpec((D_MODEL, tn), lambda i, j: (0, j)),
                             head_spec, head_spec, tab_spec, tab_spec]
        args += [w, qn, kn, cos, sins]
        body = _proj_gqa_kernel
    return pl.pallas_call(
        body,
        grid=(nt, n_out // tn),
        in_specs=in_specs,
        out_specs=pl.BlockSpec((tm, tn), lambda i, j: (i, j)),
        out_shape=jax.ShapeDtypeStruct((rows, n_out), out_dtype),
        scratch_shapes=[pltpu.VMEM((tm, D_MODEL), BF16)],
        compiler_params=_cparams(("parallel", "arbitrary")),
        name=name,
    )(*args)


def _outproj_kernel(a_ref, w_ref, b_ref, x_ref, pw_ref, g_ref, o_ref):
    y = _dot(a_ref[...], w_ref[...]) + b_ref[...]
    o_ref[...] = x_ref[...] + g_ref[...] * _rms(y, pw_ref[...])


def _outproj_mlstm_kernel(hf_ref, hb_ref, og_ref, on_ref, w_ref, b_ref, x_ref, pw_ref, g_ref, o_ref, a_scr):
    for h in range(MLSTM_HEADS):
        sl = slice(h * MLSTM_DV, (h + 1) * MLSTM_DV)
        hh = hf_ref[:, sl] + hb_ref[:, sl]
        hh = _rms(hh, on_ref[:, sl]) * _sigmoid(og_ref[:, sl].astype(F32))
        a_scr[:, sl] = hh.astype(BF16)
    y = _dot(a_scr[...], w_ref[...]) + b_ref[...]
    o_ref[...] = x_ref[...] + g_ref[...] * _rms(y, pw_ref[...])


def _out_project(a, w, bias, x, pw, g, *, tm, rows_out, name, mlstm=None):
    nt = rows_out // tm
    mi = _mod_index(tm)
    row_spec = pl.BlockSpec((tm, D_MODEL), lambda i: (i, 0))
    vec_spec = pl.BlockSpec((1, D_MODEL), lambda i: (0, 0))
    w_spec = pl.BlockSpec((D_MODEL, D_MODEL), lambda i: (0, 0))
    mod_spec = pl.BlockSpec((None, 1, D_MODEL), lambda i: (mi(i), 0, 0))
    if bias is None:
        bias = jnp.zeros((1, D_MODEL), F32)
    if mlstm is None:
        body = _outproj_kernel
        in_specs = [row_spec, w_spec, vec_spec, row_spec, vec_spec, mod_spec]
        args = [a, w, bias, x, pw, g]
        scratch = []
    else:
        hdir, z, out_norm = mlstm
        body = _outproj_mlstm_kernel
        o_gate_block = (2 * MLSTM_HEADS * MLSTM_DQK + MLSTM_HEADS * MLSTM_DV) // D_MODEL
        in_specs = [pl.BlockSpec((None, tm, D_MODEL), lambda i: (0, i, 0)),
                    pl.BlockSpec((None, tm, D_MODEL), lambda i: (1, i, 0)),
                    pl.BlockSpec((tm, D_MODEL), lambda i: (i, o_gate_block)),
                    vec_spec, w_spec, vec_spec, row_spec, vec_spec, mod_spec]
        args = [hdir, hdir, z, out_norm, w, bias, x, pw, g]
        scratch = [pltpu.VMEM((tm, D_MODEL), BF16)]
    return pl.pallas_call(
        body,
        grid=(nt,),
        in_specs=in_specs,
        out_specs=row_spec,
        out_shape=jax.ShapeDtypeStruct((rows_out, D_MODEL), F32),
        scratch_shapes=scratch,
        compiler_params=_cparams(("parallel",)),
        name=name,
    )(*args)


def _ffn_kernel(x_ref, xp_ref, xn_ref, nw_ref, sh_ref, sc_ref, g_ref, wv_ref, wg_ref, cp_ref,
                wd_ref, pw_ref, o_ref, h_scr, uv_scr, ug_scr, acc_scr, *, tm, tile0):
    i = pl.program_id(0) + tile0
    f = pl.program_id(1)

    @pl.when(f == 0)
    def _():
        def nm(xr):
            return (_rms(xr[...], nw_ref[...]) * (1.0 + sc_ref[...]) + sh_ref[...]).astype(BF16)
        h_scr[0:HALO, :] = nm(xp_ref)
        h_scr[HALO:HALO + tm, :] = nm(x_ref)
        h_scr[HALO + tm:HALO + tm + HALO, :] = nm(xn_ref)
        acc_scr[...] = jnp.zeros_like(acc_scr)

    r = i * tm + lax.broadcasted_iota(jnp.int32, (tm, 1), 0)
    has_left = jnp.where(_is_seq_start(r), 0.0, 1.0)
    has_right = jnp.where(_is_seq_start(r + 1), 0.0, 1.0)

    n_rh, n_sub = uv_scr.shape[0], uv_scr.shape[1]
    hr = tm // n_rh

    def conv(u_scr, p0, cl, rows):
        left = u_scr[HALO - 1:HALO - 1 + hr, :] * has_left[rows]
        right = u_scr[HALO + 1:HALO + 1 + hr, :] * has_right[rows]
        return (left * cp_ref[p0:p0 + 1, cl] + u_scr[HALO:HALO + hr, :] * cp_ref[p0 + 1:p0 + 2, cl]
                + right * cp_ref[p0 + 2:p0 + 3, cl] + cp_ref[p0 + 3:p0 + 4, cl])

    for rh in range(n_rh):
        hh = h_scr[rh * hr:rh * hr + hr + 2 * HALO, :]
        for cb in range(n_sub):
            cl = slice(cb * FFN_SUB, (cb + 1) * FFN_SUB)
            uv_scr[rh, cb] = _dot(hh, wv_ref[:, cl])
            ug_scr[rh, cb] = _dot(hh, wg_ref[:, cl])
    for rh in range(n_rh):
        rows = slice(rh * hr, (rh + 1) * hr)
        acts = []
        for cb in range(n_sub):
            cl = slice(cb * FFN_SUB, (cb + 1) * FFN_SUB)
            val = conv(uv_scr.at[rh, cb], 0, cl, rows)
            gate = conv(ug_scr.at[rh, cb], 4, cl, rows)
            acts.append((val * (gate * _sigmoid(gate))).astype(BF16))
        acc_scr[rows, :] += _dot(jnp.concatenate(acts, axis=1), wd_ref[...])

    @pl.when(f == pl.num_programs(1) - 1)
    def _():
        o_ref[...] = x_ref[...] + g_ref[...] * _rms(acc_scr[...], pw_ref[...])


def _ffn_kernel_into(*refs, n_in, **kw):
    _ffn_kernel(*refs[:n_in], *refs[n_in + 1:], **kw)


def _ffn_conv_params(conv_w, conv_b):
    b = conv_b[:, None, :]
    return jnp.concatenate([conv_w[:, :, :D_FF], b[:, :, :D_FF], conv_w[:, :, D_FF:], b[:, :, D_FF:]], axis=1)


def _conv_ffn(x, nw, sh, sc, g, w_up, conv_p, w_down, pw, *, layer, tm, tf, rows_out, name,
              tile0=0, n_tiles=None, into=None):
    rows = x.shape[0]
    nt = rows_out // tm if n_tiles is None else n_tiles
    nf = D_FF // tf
    mi = _mod_index(tm)
    hb = tm // HALO
    last_halo = rows // HALO - 1
    once = pl.Buffered(1) if tm > 512 else None
    vec_spec = pl.BlockSpec((1, D_MODEL), lambda i, f: (0, 0))
    mod_spec = pl.BlockSpec((None, 1, D_MODEL), lambda i, f: (mi(i + tile0), 0, 0))
    in_specs = [
        pl.BlockSpec((tm, D_MODEL), lambda i, f: (i + tile0, 0), pipeline_mode=once),
        pl.BlockSpec((HALO, D_MODEL), lambda i, f: (jnp.maximum((i + tile0) * hb - 1, 0), 0)),
        pl.BlockSpec((HALO, D_MODEL), lambda i, f: (jnp.minimum((i + tile0 + 1) * hb, last_halo), 0)),
        vec_spec, mod_spec, mod_spec, mod_spec,
        pl.BlockSpec((None, D_MODEL, tf), lambda i, f: (layer, 0, f)),
        pl.BlockSpec((None, D_MODEL, tf), lambda i, f: (layer, 0, f + nf)),
        pl.BlockSpec((None, 8, tf), lambda i, f: (layer, 0, f)),
        pl.BlockSpec((None, tf, D_MODEL), lambda i, f: (layer, f, 0)),
        vec_spec,
    ]
    args = [x, x, x, nw, sh, sc, g, w_up, w_up, conv_p, w_down, pw]
    body = functools.partial(_ffn_kernel, tm=tm, tile0=tile0)
    aliases = {}
    if into is not None:
        body = functools.partial(_ffn_kernel_into, n_in=len(args), tm=tm, tile0=tile0)
        aliases = {len(args): 0}
        in_specs = in_specs + [pl.BlockSpec(memory_space=pl.ANY)]
        args = args + [into]
    return pl.pallas_call(
        body,
        grid=(nt, nf),
        in_specs=in_specs,
        out_specs=pl.BlockSpec((tm, D_MODEL), lambda i, f: (i + tile0, 0), pipeline_mode=once),
        out_shape=jax.ShapeDtypeStruct((rows_out, D_MODEL), F32),
        input_output_aliases=aliases,
        scratch_shapes=[pltpu.VMEM((tm + 2 * HALO, D_MODEL), BF16),
                        pltpu.VMEM((FFN_ROW_BLOCKS, tf // FFN_SUB, tm // FFN_ROW_BLOCKS + 2 * HALO, FFN_SUB), F32),
                        pltpu.VMEM((FFN_ROW_BLOCKS, tf // FFN_SUB, tm // FFN_ROW_BLOCKS + 2 * HALO, FFN_SUB), F32),
                        pltpu.VMEM((tm, D_MODEL), F32)],
        compiler_params=_cparams(("parallel", "arbitrary")),
        name=name,
    )(*args)


def _exp_bf16(x):
    return jnp.exp(x.astype(BF16))


def _with_ones(v):
    return jnp.concatenate([v, jnp.ones_like(v)], axis=1)


GQA_SCORES_AHEAD = 3


def _gqa_attn_kernel(q_ref, kl_ref, vl_ref, kc_ref, vc_ref, o_ref, vla_scr, vca_scr):
    t = pl.program_id(2)
    n_lat_tiles = pl.num_programs(2) - 1
    group = GQA_HEADS // GQA_KV_HEADS
    kc = kc_ref[...]

    @pl.when(t == 0)
    def _():
        vla_scr[...] = _with_ones(vl_ref[...])
        vca_scr[...] = _with_ones(vc_ref[...])

    @pl.when(t < n_lat_tiles)
    def _():
        kl = kl_ref[...]

        def scores(gi):
            q = q_ref[:, gi * HEAD_DIM:(gi + 1) * HEAD_DIM]
            return _dot_nt(q, kc), _dot_nt(q, kl)

        ahead = GQA_SCORES_AHEAD
        pending = [scores(gi) for gi in range(min(ahead, group))]
        for gi in range(group):
            sl = slice(gi * HEAD_DIM, (gi + 1) * HEAD_DIM)
            s_c, s_l = pending.pop(0)
            if gi + ahead < group:
                pending.append(scores(gi + ahead))
            m = jnp.maximum(jnp.max(s_c, axis=1, keepdims=True), jnp.max(s_l, axis=1, keepdims=True))
            p_c = _exp_bf16(s_c - m)
            p_l = _exp_bf16(s_l - m)
            hq = p_l.shape[0] // 2
            for rs in (slice(0, hq), slice(hq, 2 * hq)):
                o = _dot(p_c[rs], vca_scr[...]) + _dot(p_l[rs], vla_scr[...])
                o_ref[rs, sl] = (o[:, :HEAD_DIM] / o[:, HEAD_DIM:HEAD_DIM + 1]).astype(o_ref.dtype)

    @pl.when(t == n_lat_tiles)
    def _():
        for gi in range(group):
            sl = slice(gi * HEAD_DIM, (gi + 1) * HEAD_DIM)
            q = q_ref[:, sl]
            s_c = _dot_nt(q, kc)
            p_c = _exp_bf16(s_c - jnp.max(s_c, axis=1, keepdims=True))
            o = _dot(p_c, vca_scr[...])
            o_ref[:, sl] = (o[:, :HEAD_DIM] / o[:, HEAD_DIM:HEAD_DIM + 1]).astype(o_ref.dtype)


def _gqa_attention(z):
    tq = SEG
    n_lat = SEQ // tq
    group_w = (GQA_HEADS // GQA_KV_HEADS) * HEAD_DIM
    k_col = GQA_HEADS
    v_col = GQA_HEADS + GQA_KV_HEADS
    ctx_blk = R_LAT // SEG

    def q_rows(b, kh, t):
        return jnp.where(t < n_lat, b * n_lat + t, ctx_blk + b)

    return pl.pallas_call(
        _gqa_attn_kernel,
        grid=(BATCH, GQA_KV_HEADS, n_lat + 1),
        in_specs=[
            pl.BlockSpec((tq, group_w), lambda b, kh, t: (q_rows(b, kh, t), kh)),
            pl.BlockSpec((SEQ, HEAD_DIM), lambda b, kh, t: (b, k_col + kh)),
            pl.BlockSpec((SEQ, HEAD_DIM), lambda b, kh, t: (b, v_col + kh)),
            pl.BlockSpec((CTX_LEN, HEAD_DIM), lambda b, kh, t: (ctx_blk + b, k_col + kh)),
            pl.BlockSpec((CTX_LEN, HEAD_DIM), lambda b, kh, t: (ctx_blk + b, v_col + kh)),
        ],
        out_specs=pl.BlockSpec((tq, group_w), lambda b, kh, t: (q_rows(b, kh, t), kh)),
        out_shape=jax.ShapeDtypeStruct((R_ALL, GQA_HEADS * HEAD_DIM), BF16),
        scratch_shapes=[pltpu.VMEM((SEQ, 2 * HEAD_DIM), BF16),
                        pltpu.VMEM((CTX_LEN, 2 * HEAD_DIM), BF16)],
        compiler_params=_cparams(("parallel", "parallel", "arbitrary")),
        name="gqa_attention",
    )(z, z, z, z, z)


def _split3(x):
    hi = x.astype(BF16)
    r1 = x - hi.astype(F32)
    mid = r1.astype(BF16)
    lo = (r1 - mid.astype(F32)).astype(BF16)
    return hi, mid, lo


def _log_sigmoid(x):
    return jnp.minimum(x, 0.0) - jnp.log(1.0 + jnp.exp(-jnp.abs(x)))


MLSTM_AUG = MLSTM_DV + 128


def _mlstm_kernel(q_ref, k_ref, v_ref, g_ref, o_ref, c_scr, m_scr):
    d = pl.program_id(1)
    c = pl.program_id(2)
    nh = MLSTM_HEADS
    L = SEG
    scale = MLSTM_DQK ** -0.5

    @pl.when(c == 0)
    def _():
        c_scr[...] = jnp.zeros_like(c_scr)
        m_scr[...] = jnp.full_like(m_scr, M_INIT)

    ri = lax.broadcasted_iota(jnp.int32, (L, L), 0)
    ci = lax.broadcasted_iota(jnp.int32, (L, L), 1)
    diff = jnp.where(d == 0, ci - ri, ri - ci)
    sees = diff <= 0
    cum_t = jnp.where(diff >= 0, 1.0, 0.0).astype(BF16)

    g = g_ref[...]
    ig = g[0:nh]
    lf = _log_sigmoid(g[nh:2 * nh])
    b = sum(_dot(p, cum_t) for p in _split3(lf))
    g_tot = jnp.sum(lf, axis=1, keepdims=True)
    m_in = m_scr[:, 0:1]
    cr = ig - b
    lane = lax.broadcasted_iota(jnp.int32, (nh, L), 1)
    pm = cr
    for step in range(L.bit_length() - 1):
        sh = 1 << step
        prev = jnp.where(lane >= sh, pltpu.roll(pm, sh, 1), NEG)
        nxt = jnp.where(lane < L - sh, pltpu.roll(pm, L - sh, 1), NEG)
        pm = jnp.maximum(pm, jnp.where(d == 0, prev, nxt))
    big_m = jnp.maximum(m_in, pm)
    w_inter = jnp.exp(m_in - big_m) * scale
    floor = jnp.exp(-(b + big_m))
    a = g_tot - b + ig
    m_loc = jnp.max(a, axis=1, keepdims=True)
    w = jnp.exp(a - m_loc)
    m_new = jnp.maximum(g_tot + m_in, m_loc)
    a_old = jnp.exp(g_tot + m_in - m_new)
    a_new = jnp.exp(m_loc - m_new)
    slab = jnp.concatenate([big_m, w_inter, floor, jnp.zeros((128 - 3 * nh, L), F32)], axis=0)
    cols = slab.T

    ones = jnp.ones((L, MLSTM_AUG - MLSTM_DV), BF16)
    for h in range(nh):
        q = q_ref[:, h * MLSTM_DQK:(h + 1) * MLSTM_DQK]
        k = k_ref[:, h * MLSTM_DQK:(h + 1) * MLSTM_DQK]
        v_aug = jnp.concatenate([v_ref[:, h * MLSTM_DV:(h + 1) * MLSTM_DV], ones], axis=1)
        e = jnp.where(sees, cr[h:h + 1, :] - cols[:, h:h + 1], NEG)
        s = _dot_nt(q, k) * (jnp.exp(e) * scale)
        c_in = c_scr[h]
        num = _dot(s.astype(BF16), v_aug) + cols[:, nh + h:nh + h + 1] * _dot(q, c_in.astype(BF16))
        den = jnp.maximum(jnp.abs(num[:, MLSTM_DV:MLSTM_DV + 1]), cols[:, 2 * nh + h:2 * nh + h + 1])
        o_ref[:, h * MLSTM_DV:(h + 1) * MLSTM_DV] = num[:, :MLSTM_DV] / den

        kw_t = (k.astype(F32).T * w[h:h + 1, :]).astype(BF16)
        c_scr[h] = a_old[h:h + 1, :] * c_in + a_new[h:h + 1, :] * _dot(kw_t, v_aug)
    m_scr[...] = jnp.broadcast_to(m_new, m_scr.shape)


def _mlstm(z, gates_t):
    n_lat = SEQ // SEG
    ctx_blk = R_LAT // SEG
    nqk = MLSTM_HEADS * MLSTM_DQK
    nv = MLSTM_HEADS * MLSTM_DV

    def rb(b, d, c):
        lat = jnp.where(d == 0, c - 1, n_lat - c)
        return jnp.where(c == 0, ctx_blk + b, b * n_lat + lat)

    return pl.pallas_call(
        _mlstm_kernel,
        grid=(BATCH, 2, n_lat + 1),
        in_specs=[
            pl.BlockSpec((SEG, nqk), lambda b, d, c: (rb(b, d, c), 0)),
            pl.BlockSpec((SEG, nqk), lambda b, d, c: (rb(b, d, c), 1)),
            pl.BlockSpec((SEG, nv), lambda b, d, c: (rb(b, d, c), 2 * nqk // nv)),
            pl.BlockSpec((2 * MLSTM_HEADS, SEG), lambda b, d, c: (d, rb(b, d, c))),
        ],
        out_specs=pl.BlockSpec((None, SEG, nv), lambda b, d, c: (d, rb(b, d, c), 0)),
        out_shape=jax.ShapeDtypeStruct((2, R_ALL, nv), F32),
        scratch_shapes=[pltpu.VMEM((MLSTM_HEADS, MLSTM_DQK, MLSTM_AUG), F32),
                        pltpu.VMEM((MLSTM_HEADS, 128), F32)],
        compiler_params=_cparams(("parallel", "parallel", "arbitrary")),
        name="mlstm",
    )(z, z, z, gates_t)


DW_ROWS = 32
DW_COLS = 512


def _dwconv_ln_kernel(u_ref, up_ref, un_ref, w_ref, b_ref, lw_ref, lb_ref, o_ref, pad_scr, conv_scr, sh_scr):
    i = pl.program_id(0)
    tm = u_ref.shape[0]
    r0 = i * tm
    keep_prev = jnp.logical_not(_is_seq_start(r0))
    keep_next = jnp.logical_not(_is_seq_start(r0 + tm))
    pad_scr[0:HALO, :] = jnp.where(keep_prev, up_ref[...].astype(F32), 0.0)
    pad_scr[HALO:HALO + tm, :] = u_ref[...].astype(F32)
    pad_scr[HALO + tm:HALO + tm + HALO, :] = jnp.where(keep_next, un_ref[...].astype(F32), 0.0)

    rc = DW_ROWS
    cw = DW_COLS
    half = CONV_WIDTH // 2
    n_sh = sh_scr.shape[1]

    def chunk(cs, carry):
        cl = pl.ds(pl.multiple_of(cs * cw, cw), cw)
        for s in range(1, 8):
            sh_scr[s] = pad_scr[s:s + n_sh, cl]
        for ib in range(tm // rc):
            base = ib * rc
            acc = jnp.zeros((rc, cw), F32) + b_ref[:, cl]
            for kk in range(CONV_WIDTH):
                al, s = divmod(HALO - half + kk, 8)
                off = base + 8 * al
                tap = pad_scr[off:off + rc, cl] if s == 0 else sh_scr[s, off:off + rc, :]
                acc = acc + tap * w_ref[kk:kk + 1, cl]
            conv_scr[base:base + rc, cl] = acc
        return carry

    lax.fori_loop(0, D_MODEL // cw, chunk, 0)
    y = conv_scr[...]
    mu = jnp.mean(y, axis=-1, keepdims=True)
    yc = y - mu
    var = jnp.mean(yc * yc, axis=-1, keepdims=True)
    yn = yc * lax.rsqrt(var + EPS) * lw_ref[...] + lb_ref[...]
    o_ref[...] = (yn * _sigmoid(yn)).astype(o_ref.dtype)


def _dwconv_ln_swish(u, w_dw, b_dw, ln_w, ln_b):
    rows = u.shape[0]
    tm = SEG
    hb = tm // HALO
    last_halo = rows // HALO - 1
    vec_spec = pl.BlockSpec((1, D_MODEL), lambda i: (0, 0))
    return pl.pallas_call(
        _dwconv_ln_kernel,
        grid=(rows // tm,),
        in_specs=[
            pl.BlockSpec((tm, D_MODEL), lambda i: (i, 0)),
            pl.BlockSpec((HALO, D_MODEL), lambda i: (jnp.maximum(i * hb - 1, 0), 0)),
            pl.BlockSpec((HALO, D_MODEL), lambda i: (jnp.minimum((i + 1) * hb, last_halo), 0)),
            pl.BlockSpec((CONV_WIDTH, D_MODEL), lambda i: (0, 0)),
            vec_spec, vec_spec, vec_spec,
        ],
        out_specs=pl.BlockSpec((tm, D_MODEL), lambda i: (i, 0)),
        out_shape=jax.ShapeDtypeStruct((rows, D_MODEL), BF16),
        scratch_shapes=[pltpu.VMEM((tm + 2 * HALO, D_MODEL), F32),
                        pltpu.VMEM((tm, D_MODEL), F32),
                        pltpu.VMEM((8, tm + 2 * HALO - 8, DW_COLS), F32)],
        compiler_params=_cparams(("parallel",)),
        name="dwconv_ln_swish",
    )(u, u, u, w_dw, b_dw, ln_w, ln_b)


NAT_Q_ROWS = 4
NAT_BAND_ROWS = 12
NAT_TQ = NAT_Q_ROWS * GRID_W
NAT_BAND = NAT_BAND_ROWS * GRID_W


NAT_HEADS_PER_STEP = 4


def _nat_kernel(q_ref, kl_ref, vl_ref, kc_ref, vc_ref, bias_ref, o_ref, vla_scr, vca_scr):
    gq = pl.program_id(2)
    max_start = (SEQ - NAT_BAND) // NAT_TQ
    start = pl.multiple_of(jnp.clip(gq - 1, 0, max_start) * NAT_TQ, NAT_TQ)
    hd = NA_HEAD_DIM

    @pl.when(gq == 0)
    def _():
        for hh in range(NAT_HEADS_PER_STEP):
            sl = slice(hh * hd, (hh + 1) * hd)
            vla_scr[:, 2 * hh * hd:2 * (hh + 1) * hd] = _with_ones(vl_ref[:, sl])
            vca_scr[:, 2 * hh * hd:2 * (hh + 1) * hd] = _with_ones(vc_ref[:, sl])

    def scores(hh):
        sl = slice(hh * hd, (hh + 1) * hd)
        q = q_ref[:, sl]
        return _dot_nt(q, kl_ref[pl.ds(start, NAT_BAND), sl]) + bias_ref[hh], _dot_nt(q, kc_ref[:, sl])

    s_next = scores(0)
    for hh in range(NAT_HEADS_PER_STEP):
        sl = slice(hh * hd, (hh + 1) * hd)
        sl2 = slice(2 * hh * hd, 2 * (hh + 1) * hd)
        s_n, s_c = s_next
        if hh + 1 < NAT_HEADS_PER_STEP:
            s_next = scores(hh + 1)
        m = jnp.maximum(jnp.max(s_n, axis=1, keepdims=True), jnp.max(s_c, axis=1, keepdims=True))
        p_n = _exp_bf16(s_n - m)
        p_c = _exp_bf16(s_c - m)
        o = _dot(p_n, vla_scr[pl.ds(start, NAT_BAND), sl2]) + _dot(p_c, vca_scr[:, sl2])
        o_ref[:, sl] = (o[:, :hd] / o[:, hd:hd + 1]).astype(o_ref.dtype)


def _nat_bias_blocks():
    rows = SEQ // GRID_W
    n_groups = rows // NAT_Q_ROWS
    q_col = np.arange(GRID_W)
    c_start = np.clip(q_col - NA_COLS // 2, 0, GRID_W - NA_COLS)
    dc = q_col[None, :] - c_start[:, None]
    col_ok = (dc >= 0) & (dc < NA_COLS)
    masked = 2 * NA_ROWS - 1
    block = np.full((3, NAT_Q_ROWS, NAT_BAND_ROWS), masked, np.int64)
    for var, gq in enumerate((0, 1, n_groups - 1)):
        lo = int(np.clip(gq - 1, 0, (SEQ - NAT_BAND) // NAT_TQ)) * NAT_Q_ROWS
        for qr in range(NAT_Q_ROWS):
            q_row = gq * NAT_Q_ROWS + qr
            r_start = int(np.clip(q_row - NA_ROWS // 2, 0, rows - NA_ROWS))
            for kr in range(NAT_BAND_ROWS):
                k_row = lo + kr
                if 0 <= k_row - r_start < NA_ROWS:
                    block[var, qr, kr] = k_row - q_row + (NA_ROWS - 1)
    return col_ok, block


def _nat_bias(rpb):
    n_r, n_c = 2 * NA_ROWS - 1, 2 * NA_COLS - 1
    w = GRID_W
    col_ok, block = _nat_bias_blocks()
    lead = w - NA_COLS
    vext = jnp.pad(rpb, ((0, 0), (0, 0), (lead, 2 * w - lead - n_c)))
    skew = jnp.broadcast_to(vext[:, :, None, :], (NA_HEADS, n_r, w, 2 * w)).reshape(NA_HEADS, n_r, 2 * w * w)
    skew = skew[:, :, :w * (2 * w - 1)].reshape(NA_HEADS, n_r, w, 2 * w - 1)
    tb = jnp.where(col_ok[None, None], skew[..., w - 1:], NEG)
    tb = jnp.concatenate([tb, jnp.full((NA_HEADS, 1, w, w), NEG, F32)], axis=1)
    strips = [jnp.concatenate([tb[:, int(d)] for d in block[var, qr]], axis=-1)
              for var in range(3) for qr in range(NAT_Q_ROWS)]
    return jnp.stack(strips, axis=1).reshape(NA_HEADS, 3, NAT_TQ, NAT_BAND)


def _nat_attention(z, rpb):
    n_groups = SEQ // NAT_TQ
    ctx_blk = R_LAT // SEG
    bias = _nat_bias(rpb)
    hps = NAT_HEADS_PER_STEP
    gw = hps * NA_HEAD_DIM

    def variant(gq):
        return (gq > 0).astype(jnp.int32) + (gq == n_groups - 1).astype(jnp.int32)

    k_col = NA_HEADS // hps
    v_col = 2 * NA_HEADS // hps
    return pl.pallas_call(
        _nat_kernel,
        grid=(BATCH, NA_HEADS // hps, n_groups),
        in_specs=[
            pl.BlockSpec((NAT_TQ, gw), lambda b, h, gq: (b * n_groups + gq, h)),
            pl.BlockSpec((SEQ, gw), lambda b, h, gq: (b, k_col + h)),
            pl.BlockSpec((SEQ, gw), lambda b, h, gq: (b, v_col + h)),
            pl.BlockSpec((CTX_LEN, gw), lambda b, h, gq: (ctx_blk + b, k_col + h)),
            pl.BlockSpec((CTX_LEN, gw), lambda b, h, gq: (ctx_blk + b, v_col + h)),
            pl.BlockSpec((hps, None, NAT_TQ, NAT_BAND), lambda b, h, gq: (h, variant(gq), 0, 0)),
        ],
        out_specs=pl.BlockSpec((NAT_TQ, gw), lambda b, h, gq: (b * n_groups + gq, h)),
        out_shape=jax.ShapeDtypeStruct((R_LAT, NA_HEADS * NA_HEAD_DIM), BF16),
        scratch_shapes=[pltpu.VMEM((SEQ, 2 * gw), BF16),
                        pltpu.VMEM((CTX_LEN, 2 * gw), BF16)],
        compiler_params=_cparams(("parallel", "parallel", "arbitrary")),
        name="nat_attention",
    )(z, z, z, z, z, bias)


def _rope_tables():
    pos = jnp.arange(SEQ)
    quarter = HEAD_DIM // 4
    freqs = ROPE_THETA ** (-jnp.arange(quarter, dtype=F32) / quarter)

    def cs(p):
        ang = p.astype(F32)[:, None] * freqs[None, :]
        return jnp.cos(ang), jnp.sin(ang)

    cr, sr = cs(pos // GRID_W)
    cc, sc = cs(pos % GRID_W)
    cos = jnp.concatenate([cr, cr, cc, cc], axis=1)
    sins = jnp.concatenate([-sr, sr, -sc, sc], axis=1)
    n_ctx = BATCH * CTX_LEN
    cos = jnp.concatenate([cos] * BATCH + [jnp.ones((n_ctx, HEAD_DIM), F32)], axis=0)
    sins = jnp.concatenate([sins] * BATCH + [jnp.zeros((n_ctx, HEAD_DIM), F32)], axis=0)
    return cos, sins


def kernel(x, c, ctx, c_ctx, mod_w, mod_b, norm_pre_mix, norm_post_mix, norm_pre_ffn, norm_post_ffn, ffn_w_up, ffn_conv_w, ffn_conv_b, ffn_w_down, gqa_w_in, gqa_q_norm, gqa_k_norm, gqa_w_out, mlstm_w_in, mlstm_b_gate, mlstm_out_norm, mlstm_w_out, conv_w_pw1, conv_b_pw1, conv_w_dw, conv_b_dw, conv_ln_w, conv_ln_b, conv_w_pw2, conv_b_pw2, nat_w_in, nat_rpb, nat_w_out):
    tm = 512
    xs = jnp.concatenate([x.reshape(R_LAT, D_MODEL), ctx.reshape(BATCH * CTX_LEN, D_MODEL)], axis=0)
    a8 = jnp.concatenate([c, c_ctx[None, :], jnp.zeros((8 - BATCH - 1, D_MODEL), F32)], axis=0)
    mods = _modulation(a8, mod_w, mod_b)
    mods = mods[:, :BATCH + 1].reshape(DEPTH, BATCH + 1, MOD_CHUNKS, 1, D_MODEL)
    cos, sins = _rope_tables()
    w_up_bf16 = ffn_w_up.astype(BF16)
    w_down_bf16 = ffn_w_down.astype(BF16)
    conv_p = _ffn_conv_params(ffn_conv_w, ffn_conv_b)

    def row(v):
        return v.reshape(1, -1)

    for i in range(DEPTH):
        kind = i % 4
        last = i == DEPTH - 1
        rows_out = R_LAT if last else R_ALL
        sh1, sc1, g1, sh2, sc2, g2 = [mods[i, :, m] for m in range(MOD_CHUNKS)]
        pre = row(norm_pre_mix[i])
        post = row(norm_post_mix[i])
        x_dead = xs
        if kind == 0:
            z = _project(xs, pre, sh1, sc1, gqa_w_in[0].astype(BF16), mode="gqa", tm=tm, tn=1024,
                         extra=(row(gqa_q_norm[0]), row(gqa_k_norm[0]), cos, sins), name="gqa_in")
            a = _gqa_attention(z)
            xs = _out_project(a, gqa_w_out[0].astype(BF16), None, xs, post, g1, tm=tm, rows_out=rows_out,
                              name="gqa_out")
        elif kind == 1:
            n_main = 2 * MLSTM_HEADS * MLSTM_DQK + 2 * MLSTM_HEADS * MLSTM_DV
            n_gate = 4 * MLSTM_HEADS
            w_in = mlstm_w_in[0]
            w_g = jnp.pad(w_in[:, n_main:], ((0, 0), (0, 128 - n_gate))).astype(BF16)
            b_g = jnp.pad(mlstm_b_gate[0], (0, 128 - n_gate)).reshape(1, 128)
            z, gates_t = _project(xs, pre, sh1, sc1, w_in[:, :n_main].astype(BF16), mode="mlstm", tm=tm, tn=2048,
                                  extra=(w_g, b_g), name="mlstm_in")
            hdir = _mlstm(z, gates_t)
            xs = _out_project(None, mlstm_w_out[0].astype(BF16), None, xs, post, g1, tm=SEG, rows_out=rows_out,
                              name="mlstm_out", mlstm=(hdir, z, row(mlstm_out_norm[0])))
        elif kind == 2:
            u = _project(xs, pre, sh1, sc1, conv_w_pw1[0].astype(BF16), mode="glu", tm=tm, tn=1024,
                         bias=row(conv_b_pw1[0]), name="conformer_pw1")
            a = _dwconv_ln_swish(u, conv_w_dw[0], row(conv_b_dw[0]), row(conv_ln_w[0]), row(conv_ln_b[0]))
            xs = _out_project(a, conv_w_pw2[0].astype(BF16), row(conv_b_pw2[0]), xs, post, g1, tm=tm,
                              rows_out=rows_out, name="conformer_pw2")
        else:
            nq = NA_HEADS * NA_HEAD_DIM
            z = _project(xs, pre, sh1, sc1, nat_w_in[0].astype(BF16), mode="plain", tm=tm, tn=2048,
                         n_scaled=nq // 2048, scale=NA_HEAD_DIM ** -0.5, name="nat_in")
            a = _nat_attention(z, nat_rpb[0])
            xs = _out_project(a, nat_w_out[0].astype(BF16), None, xs, post, g1, tm=tm, rows_out=rows_out,
                              name="nat_out")
        ffn_args = (xs, row(norm_pre_ffn[i]), sh2, sc2, g2, w_up_bf16, conv_p, w_down_bf16, row(norm_post_ffn[i]))
        x_new = _conv_ffn(*ffn_args, layer=i, tm=FFN_TM_LATENT, tf=FFN_TF, rows_out=rows_out,
                          n_tiles=R_LAT // FFN_TM_LATENT, into=None if last else x_dead, name="conv_ffn")
        if not last:
            x_new = _conv_ffn(*ffn_args, layer=i, tm=tm, tf=FFN_TF, rows_out=rows_out, tile0=R_LAT // tm,
                              n_tiles=(R_ALL - R_LAT) // tm, into=x_new, name="conv_ffn_ctx")
        xs = x_new
    return xs.reshape(BATCH, SEQ, D_MODEL)
```

```python
import functools

import numpy as np
import jax
import jax.numpy as jnp
from jax import lax
from jax.experimental import pallas as pl
from jax.experimental.pallas import tpu as pltpu

F32 = jnp.float32
BF16 = jnp.bfloat16

D_MODEL = 2048
BATCH = 2
SEQ = 4096
DEPTH = 4
GRID_W = 64
CTX_LEN = 256
EPS = 1e-6
MOD_CHUNKS = 6
D_FF = 5632
GQA_HEADS = 16
GQA_KV_HEADS = 4
HEAD_DIM = 128
ROPE_THETA = 10000.0
MLSTM_HEADS = 8
MLSTM_DQK = 128
MLSTM_DV = 256
M_INIT = -1e30
CONV_WIDTH = 31
NA_HEADS = 16
NA_HEAD_DIM = 128
NA_ROWS = 8
NA_COLS = 16

R_LAT = BATCH * SEQ
R_ALL = R_LAT + BATCH * CTX_LEN
NEG = -1e30

VMEM_LIMIT = 56 * 1024 * 1024
SEG = 256
HALO = 16
FFN_SUB = 256
FFN_ROW_BLOCKS = 2
FFN_TF = 512
FFN_TM_LATENT = 1024


def _cparams(sem, flags=None):
    return pltpu.CompilerParams(dimension_semantics=sem, vmem_limit_bytes=VMEM_LIMIT, flags=flags)


def _sigmoid(x):
    return 1.0 / (1.0 + jnp.exp(-x))


def _rms(x, w):
    return x * lax.rsqrt(jnp.mean(x * x, axis=-1, keepdims=True) + EPS) * w


def _dot(a, b):
    return jnp.dot(a, b, preferred_element_type=F32)


def _dot_nt(a, b):
    return lax.dot_general(a, b, (((1,), (1,)), ((), ())), preferred_element_type=F32)


def _dot_tn(a, b):
    return lax.dot_general(a, b, (((0,), (0,)), ((), ())), preferred_element_type=F32)


def _is_seq_start(r):
    return (r & jnp.where(r < R_LAT, SEQ - 1, CTX_LEN - 1)) == 0


def _mod_index(tm):
    per_batch = SEQ // tm
    return lambda i: jnp.minimum(i // per_batch, BATCH)


def _mod_kernel(a_ref, w_ref, b_ref, o_ref):
    a = a_ref[...]
    s = (a * _sigmoid(a)).astype(BF16)
    o_ref[...] = _dot(s, w_ref[...].astype(BF16)) + b_ref[...]


def _modulation(a8, mod_w, mod_b):
    tn = 1024
    n = MOD_CHUNKS * D_MODEL
    return pl.pallas_call(
        _mod_kernel,
        grid=(DEPTH, n // tn),
        in_specs=[
            pl.BlockSpec((8, D_MODEL), lambda l, j: (0, 0)),
            pl.BlockSpec((None, D_MODEL, tn), lambda l, j: (l, 0, j)),
            pl.BlockSpec((None, 1, tn), lambda l, j: (l, 0, j)),
        ],
        out_specs=pl.BlockSpec((None, 8, tn), lambda l, j: (l, 0, j)),
        out_shape=jax.ShapeDtypeStruct((DEPTH, 8, n), F32),
        compiler_params=_cparams(("parallel", "parallel")),
        name="modulation",
    )(a8, mod_w, mod_b.reshape(DEPTH, 1, n))


NORM_ROWS = 16


def _norm_mod_rows(x_ref, dst_ref, dst0, nw_ref, sh_ref, sc_ref):
    n_rows = x_ref.shape[0]
    step = min(NORM_ROWS, n_rows)
    for r in range(0, n_rows, step):
        h = _rms(x_ref[r:r + step, :], nw_ref[...]) * (1.0 + sc_ref[...]) + sh_ref[...]
        dst_ref[dst0 + r:dst0 + r + step, :] = h.astype(BF16)


def _gated_residual_rows(o_ref, x_ref, g_ref, y_of, pw_ref):
    n_rows = o_ref.shape[0]
    for r in range(0, n_rows, NORM_ROWS):
        rows = slice(r, r + NORM_ROWS)
        o_ref[rows, :] = x_ref[rows, :] + g_ref[...] * _rms(y_of(rows), pw_ref[...])


def _norm_mod_to_scratch(x_ref, nw_ref, sh_ref, sc_ref, h_scr):
    @pl.when(pl.program_id(1) == 0)
    def _():
        _norm_mod_rows(x_ref, h_scr, 0, nw_ref, sh_ref, sc_ref)


def _proj_plain_kernel(x_ref, nw_ref, sh_ref, sc_ref, w_ref, b_ref, o_ref, h_scr, *, n_scaled, scale):
    _norm_mod_to_scratch(x_ref, nw_ref, sh_ref, sc_ref, h_scr)
    y = _dot(h_scr[...], w_ref[...]) + b_ref[...]
    if n_scaled:
        y = y * jnp.where(pl.program_id(1) < n_scaled, scale, 1.0).astype(F32)
    o_ref[...] = y.astype(o_ref.dtype)


def _proj_mlstm_kernel(x_ref, nw_ref, sh_ref, sc_ref, w_ref, wg_ref, bg_ref, o_ref, gt_ref, h_scr):
    _norm_mod_to_scratch(x_ref, nw_ref, sh_ref, sc_ref, h_scr)

    @pl.when(pl.program_id(1) == 0)
    def _():
        gt_ref[...] = (_dot(h_scr[...], wg_ref[...]) + bg_ref[...]).T

    o_ref[...] = _dot(h_scr[...], w_ref[...]).astype(o_ref.dtype)


def _proj_glu_kernel(x_ref, nw_ref, sh_ref, sc_ref, wa_ref, wg_ref, ba_ref, bg_ref, o_ref, h_scr):
    _norm_mod_to_scratch(x_ref, nw_ref, sh_ref, sc_ref, h_scr)
    h = h_scr[...]
    a = _dot(h, wa_ref[...]) + ba_ref[...]
    g = _dot(h, wg_ref[...]) + bg_ref[...]
    o_ref[...] = (a * _sigmoid(g)).astype(o_ref.dtype)


def _rope(y, cos, sins, first_half):
    partner = jnp.where(first_half, pltpu.roll(y, 96, 1), pltpu.roll(y, 32, 1))
    return y * cos + partner * sins


def _proj_gqa_kernel(x_ref, nw_ref, sh_ref, sc_ref, w_ref, qn_ref, kn_ref, cos_ref, sin_ref, o_ref, h_scr):
    _norm_mod_to_scratch(x_ref, nw_ref, sh_ref, sc_ref, h_scr)
    j = pl.program_id(1)
    tn = o_ref.shape[1]
    nq = GQA_HEADS * HEAD_DIM
    nkv = GQA_KV_HEADS * HEAD_DIM
    sub = 2 * HEAD_DIM
    cos = cos_ref[...]
    sins = sin_ref[...]
    lane = lax.broadcasted_iota(jnp.int32, cos.shape, 1)
    first_half = (lane & 63) < 32
    h = h_scr[...]
    for cb in range(tn // sub):
        col0 = j * tn + cb * sub
        is_q = col0 < nq
        is_v = col0 >= nq + nkv
        nw = jnp.where(is_q, qn_ref[...], kn_ref[...])
        scale = jnp.where(is_q, HEAD_DIM ** -0.5, 1.0).astype(F32)
        y = _dot(h, w_ref[:, cb * sub:(cb + 1) * sub])
        for s in range(sub // HEAD_DIM):
            ys = y[:, s * HEAD_DIM:(s + 1) * HEAD_DIM]
            rot = _rope(_rms(ys, nw) * scale, cos, sins, first_half)
            c0 = cb * sub + s * HEAD_DIM
            o_ref[:, c0:c0 + HEAD_DIM] = jnp.where(is_v, ys, rot).astype(o_ref.dtype)


def _project(x, nw, sh, sc, w, *, mode, tm, tn, out_dtype=BF16, bias=None, n_scaled=0, scale=1.0,
             extra=None, name):
    rows = x.shape[0]
    n = w.shape[1]
    nt = rows // tm
    mi = _mod_index(tm)
    row_spec = pl.BlockSpec((tm, D_MODEL), lambda i, j: (i, 0))
    vec_spec = pl.BlockSpec((1, D_MODEL), lambda i, j: (0, 0))
    mod_spec = pl.BlockSpec((None, 1, D_MODEL), lambda i, j: (mi(i), 0, 0))
    common = [row_spec, vec_spec, mod_spec, mod_spec]
    args = [x, nw, sh, sc]
    if mode == "plain":
        n_out = n
        if bias is None:
            bias = jnp.zeros((1, n), F32)
        in_specs = common + [pl.BlockSpec((D_MODEL, tn), lambda i, j: (0, j)),
                             pl.BlockSpec((1, tn), lambda i, j: (0, j))]
        args += [w, bias]
        body = functools.partial(_proj_plain_kernel, n_scaled=n_scaled, scale=scale)
    elif mode == "glu":
        n_out = n // 2
        off = n_out // tn
        in_specs = common + [pl.BlockSpec((D_MODEL, tn), lambda i, j: (0, j)),
                             pl.BlockSpec((D_MODEL, tn), lambda i, j: (0, j + off)),
                             pl.BlockSpec((1, tn), lambda i, j: (0, j)),
                             pl.BlockSpec((1, tn), lambda i, j: (0, j + off))]
        args += [w, w, bias, bias]
        body = _proj_glu_kernel
    elif mode == "mlstm":
        n_out = n
        w_g, b_g = extra
        ng = w_g.shape[1]
        in_specs = common + [pl.BlockSpec((D_MODEL, tn), lambda i, j: (0, j)),
                             pl.BlockSpec((D_MODEL, ng), lambda i, j: (0, 0)),
                             pl.BlockSpec((1, ng), lambda i, j: (0, 0))]
        args += [w, w_g, b_g]
        return pl.pallas_call(
            _proj_mlstm_kernel,
            grid=(nt, n_out // tn),
            in_specs=in_specs,
            out_specs=[pl.BlockSpec((tm, tn), lambda i, j: (i, j)),
                       pl.BlockSpec((ng, tm), lambda i, j: (0, i))],
            out_shape=[jax.ShapeDtypeStruct((rows, n_out), out_dtype),
                       jax.ShapeDtypeStruct((ng, rows), F32)],
            scratch_shapes=[pltpu.VMEM((tm, D_MODEL), BF16)],
            compiler_params=_cparams(("parallel", "arbitrary")),
            name=name,
        )(*args)
    else:
        n_out = n
        qn, kn, cos, sins = extra
        head_spec = pl.BlockSpec((1, HEAD_DIM), lambda i, j: (0, 0))
        tab_spec = pl.BlockSpec((tm, HEAD_DIM), lambda i, j: (i, 0))
        in_specs = common + [pl.BlockSpec((D_MODEL, tn), lambda i, j: (0, j)),
                             head_spec, head_spec, tab_spec, tab_spec]
        args += [w, qn, kn, cos, sins]
        body = _proj_gqa_kernel
    return pl.pallas_call(
        body,
        grid=(nt, n_out // tn),
        in_specs=in_specs,
        out_specs=pl.BlockSpec((tm, tn), lambda i, j: (i, j)),
        out_shape=jax.ShapeDtypeStruct((rows, n_out), out_dtype),
        scratch_shapes=[pltpu.VMEM((tm, D_MODEL), BF16)],
        compiler_params=_cparams(("parallel", "arbitrary")),
        name=name,
    )(*args)


def _outproj_kernel(a_ref, w_ref, b_ref, x_ref, pw_ref, g_ref, o_ref):
    y = _dot(a_ref[...], w_ref[...]) + b_ref[...]
    o_ref[...] = x_ref[...] + g_ref[...] * _rms(y, pw_ref[...])


def _outproj_mlstm_kernel(hf_ref, hb_ref, og_ref, on_ref, w_ref, b_ref, x_ref, pw_ref, g_ref, o_ref, a_scr):
    for h in range(MLSTM_HEADS):
        sl = slice(h * MLSTM_DV, (h + 1) * MLSTM_DV)
        hh = hf_ref[:, sl] + hb_ref[:, sl]
        hh = _rms(hh, on_ref[:, sl]) * _sigmoid(og_ref[:, sl].astype(F32))
        a_scr[:, sl] = hh.astype(BF16)
    y = _dot(a_scr[...], w_ref[...]) + b_ref[...]
    o_ref[...] = x_ref[...] + g_ref[...] * _rms(y, pw_ref[...])


def _out_project(a, w, bias, x, pw, g, *, tm, rows_out, name, mlstm=None):
    nt = rows_out // tm
    mi = _mod_index(tm)
    row_spec = pl.BlockSpec((tm, D_MODEL), lambda i: (i, 0))
    vec_spec = pl.BlockSpec((1, D_MODEL), lambda i: (0, 0))
    w_spec = pl.BlockSpec((D_MODEL, D_MODEL), lambda i: (0, 0))
    mod_spec = pl.BlockSpec((None, 1, D_MODEL), lambda i: (mi(i), 0, 0))
    if bias is None:
        bias = jnp.zeros((1, D_MODEL), F32)
    if mlstm is None:
        body = _outproj_kernel
        in_specs = [row_spec, w_spec, vec_spec, row_spec, vec_spec, mod_spec]
        args = [a, w, bias, x, pw, g]
        scratch = []
    else:
        hdir, z, out_norm = mlstm
        body = _outproj_mlstm_kernel
        o_gate_block = (2 * MLSTM_HEADS * MLSTM_DQK + MLSTM_HEADS * MLSTM_DV) // D_MODEL
        in_specs = [pl.BlockSpec((None, tm, D_MODEL), lambda i: (0, i, 0)),
                    pl.BlockSpec((None, tm, D_MODEL), lambda i: (1, i, 0)),
                    pl.BlockSpec((tm, D_MODEL), lambda i: (i, o_gate_block)),
                    vec_spec, w_spec, vec_spec, row_spec, vec_spec, mod_spec]
        args = [hdir, hdir, z, out_norm, w, bias, x, pw, g]
        scratch = [pltpu.VMEM((tm, D_MODEL), BF16)]
    return pl.pallas_call(
        body,
        grid=(nt,),
        in_specs=in_specs,
        out_specs=row_spec,
        out_shape=jax.ShapeDtypeStruct((rows_out, D_MODEL), F32),
        scratch_shapes=scratch,
        compiler_params=_cparams(("parallel",)),
        name=name,
    )(*args)


def _ffn_kernel(x_ref, xp_ref, xn_ref, nw_ref, sh_ref, sc_ref, g_ref, wv_ref, wg_ref, cp_ref,
                wd_ref, pw_ref, o_ref, h_scr, uv_scr, ug_scr, acc_scr, *, tm, tile0):
    i = pl.program_id(0) + tile0
    f = pl.program_id(1)

    @pl.when(f == 0)
    def _():
        _norm_mod_rows(xp_ref, h_scr, 0, nw_ref, sh_ref, sc_ref)
        _norm_mod_rows(x_ref, h_scr, HALO, nw_ref, sh_ref, sc_ref)
        _norm_mod_rows(xn_ref, h_scr, HALO + tm, nw_ref, sh_ref, sc_ref)
        acc_scr[...] = jnp.zeros_like(acc_scr)

    r = i * tm + lax.broadcasted_iota(jnp.int32, (tm, 1), 0)
    has_left = jnp.where(_is_seq_start(r), 0.0, 1.0)
    has_right = jnp.where(_is_seq_start(r + 1), 0.0, 1.0)

    n_rh, n_sub = uv_scr.shape[0], uv_scr.shape[1]
    hr = tm // n_rh

    def conv(u_scr, p0, cl, rows):
        left = u_scr[HALO - 1:HALO - 1 + hr, :] * has_left[rows]
        right = u_scr[HALO + 1:HALO + 1 + hr, :] * has_right[rows]
        return (left * cp_ref[p0:p0 + 1, cl] + u_scr[HALO:HALO + hr, :] * cp_ref[p0 + 1:p0 + 2, cl]
                + right * cp_ref[p0 + 2:p0 + 3, cl] + cp_ref[p0 + 3:p0 + 4, cl])

    for rh in range(n_rh):
        hh = h_scr[rh * hr:rh * hr + hr + 2 * HALO, :]
        for cb in range(n_sub):
            cl = slice(cb * FFN_SUB, (cb + 1) * FFN_SUB)
            uv_scr[rh, cb] = _dot(hh, wv_ref[:, cl])
            ug_scr[rh, cb] = _dot(hh, wg_ref[:, cl])
    for rh in range(n_rh):
        rows = slice(rh * hr, (rh + 1) * hr)
        acts = []
        for cb in range(n_sub):
            cl = slice(cb * FFN_SUB, (cb + 1) * FFN_SUB)
            val = conv(uv_scr.at[rh, cb], 0, cl, rows)
            gate = conv(ug_scr.at[rh, cb], 4, cl, rows)
            acts.append((val * (gate * _sigmoid(gate))).astype(BF16))
        acc_scr[rows, :] += _dot(jnp.concatenate(acts, axis=1), wd_ref[...])

    @pl.when(f == pl.num_programs(1) - 1)
    def _():
        _gated_residual_rows(o_ref, x_ref, g_ref, lambda rows: acc_scr[rows, :], pw_ref)


def _ffn_kernel_into(*refs, n_in, **kw):
    _ffn_kernel(*refs[:n_in], *refs[n_in + 1:], **kw)


def _ffn_conv_params(conv_w, conv_b):
    b = conv_b[:, None, :]
    return jnp.concatenate([conv_w[:, :, :D_FF], b[:, :, :D_FF], conv_w[:, :, D_FF:], b[:, :, D_FF:]], axis=1)


def _conv_ffn(x, nw, sh, sc, g, w_up, conv_p, w_down, pw, *, layer, tm, tf, rows_out, name,
              tile0=0, n_tiles=None, into=None):
    rows = x.shape[0]
    nt = rows_out // tm if n_tiles is None else n_tiles
    nf = D_FF // tf
    mi = _mod_index(tm)
    hb = tm // HALO
    last_halo = rows // HALO - 1
    once = pl.Buffered(1) if tm > 512 else None
    vec_spec = pl.BlockSpec((1, D_MODEL), lambda i, f: (0, 0))
    mod_spec = pl.BlockSpec((None, 1, D_MODEL), lambda i, f: (mi(i + tile0), 0, 0))
    in_specs = [
        pl.BlockSpec((tm, D_MODEL), lambda i, f: (i + tile0, 0), pipeline_mode=once),
        pl.BlockSpec((HALO, D_MODEL), lambda i, f: (jnp.maximum((i + tile0) * hb - 1, 0), 0)),
        pl.BlockSpec((HALO, D_MODEL), lambda i, f: (jnp.minimum((i + tile0 + 1) * hb, last_halo), 0)),
        vec_spec, mod_spec, mod_spec, mod_spec,
        pl.BlockSpec((None, D_MODEL, tf), lambda i, f: (layer, 0, f)),
        pl.BlockSpec((None, D_MODEL, tf), lambda i, f: (layer, 0, f + nf)),
        pl.BlockSpec((None, 8, tf), lambda i, f: (layer, 0, f)),
        pl.BlockSpec((None, tf, D_MODEL), lambda i, f: (layer, f, 0)),
        vec_spec,
    ]
    args = [x, x, x, nw, sh, sc, g, w_up, w_up, conv_p, w_down, pw]
    body = functools.partial(_ffn_kernel, tm=tm, tile0=tile0)
    aliases = {}
    if into is not None:
        body = functools.partial(_ffn_kernel_into, n_in=len(args), tm=tm, tile0=tile0)
        aliases = {len(args): 0}
        in_specs = in_specs + [pl.BlockSpec(memory_space=pl.ANY)]
        args = args + [into]
    return pl.pallas_call(
        body,
        grid=(nt, nf),
        in_specs=in_specs,
        out_specs=pl.BlockSpec((tm, D_MODEL), lambda i, f: (i + tile0, 0), pipeline_mode=once),
        out_shape=jax.ShapeDtypeStruct((rows_out, D_MODEL), F32),
        input_output_aliases=aliases,
        scratch_shapes=[pltpu.VMEM((tm + 2 * HALO, D_MODEL), BF16),
                        pltpu.VMEM((FFN_ROW_BLOCKS, tf // FFN_SUB, tm // FFN_ROW_BLOCKS + 2 * HALO, FFN_SUB), F32),
                        pltpu.VMEM((FFN_ROW_BLOCKS, tf // FFN_SUB, tm // FFN_ROW_BLOCKS + 2 * HALO, FFN_SUB), F32),
                        pltpu.VMEM((tm, D_MODEL), F32)],
        compiler_params=_cparams(("parallel", "arbitrary")),
        name=name,
    )(*args)


def _exp_bf16(x):
    return jnp.exp(x.astype(BF16))


def _with_ones(v):
    return jnp.concatenate([v, jnp.ones_like(v)], axis=1)


GQA_SCORES_AHEAD = 3


def _gqa_attn_kernel(q_ref, kl_ref, vl_ref, kc_ref, vc_ref, o_ref, vla_scr, vca_scr):
    t = pl.program_id(2)
    n_lat_tiles = pl.num_programs(2) - 1
    group = GQA_HEADS // GQA_KV_HEADS
    kc = kc_ref[...]

    @pl.when(t == 0)
    def _():
        vla_scr[...] = _with_ones(vl_ref[...])
        vca_scr[...] = _with_ones(vc_ref[...])

    @pl.when(t < n_lat_tiles)
    def _():
        kl = kl_ref[...]

        def scores(gi):
            q = q_ref[:, gi * HEAD_DIM:(gi + 1) * HEAD_DIM]
            return _dot_nt(q, kc), _dot_nt(q, kl)

        ahead = GQA_SCORES_AHEAD
        pending = [scores(gi) for gi in range(min(ahead, group))]
        for gi in range(group):
            sl = slice(gi * HEAD_DIM, (gi + 1) * HEAD_DIM)
            s_c, s_l = pending.pop(0)
            if gi + ahead < group:
                pending.append(scores(gi + ahead))
            m = jnp.maximum(jnp.max(s_c, axis=1, keepdims=True), jnp.max(s_l, axis=1, keepdims=True))
            p_c = _exp_bf16(s_c - m)
            p_l = _exp_bf16(s_l - m)
            hq = p_l.shape[0] // 2
            for rs in (slice(0, hq), slice(hq, 2 * hq)):
                o = _dot(p_c[rs], vca_scr[...]) + _dot(p_l[rs], vla_scr[...])
                o_ref[rs, sl] = (o[:, :HEAD_DIM] / o[:, HEAD_DIM:HEAD_DIM + 1]).astype(o_ref.dtype)

    @pl.when(t == n_lat_tiles)
    def _():
        for gi in range(group):
            sl = slice(gi * HEAD_DIM, (gi + 1) * HEAD_DIM)
            q = q_ref[:, sl]
            s_c = _dot_nt(q, kc)
            p_c = _exp_bf16(s_c - jnp.max(s_c, axis=1, keepdims=True))
            o = _dot(p_c, vca_scr[...])
            o_ref[:, sl] = (o[:, :HEAD_DIM] / o[:, HEAD_DIM:HEAD_DIM + 1]).astype(o_ref.dtype)


def _gqa_attention(z):
    tq = SEG
    n_lat = SEQ // tq
    group_w = (GQA_HEADS // GQA_KV_HEADS) * HEAD_DIM
    k_col = GQA_HEADS
    v_col = GQA_HEADS + GQA_KV_HEADS
    ctx_blk = R_LAT // SEG

    def q_rows(b, kh, t):
        return jnp.where(t < n_lat, b * n_lat + t, ctx_blk + b)

    return pl.pallas_call(
        _gqa_attn_kernel,
        grid=(BATCH, GQA_KV_HEADS, n_lat + 1),
        in_specs=[
            pl.BlockSpec((tq, group_w), lambda b, kh, t: (q_rows(b, kh, t), kh)),
            pl.BlockSpec((SEQ, HEAD_DIM), lambda b, kh, t: (b, k_col + kh)),
            pl.BlockSpec((SEQ, HEAD_DIM), lambda b, kh, t: (b, v_col + kh)),
            pl.BlockSpec((CTX_LEN, HEAD_DIM), lambda b, kh, t: (ctx_blk + b, k_col + kh)),
            pl.BlockSpec((CTX_LEN, HEAD_DIM), lambda b, kh, t: (ctx_blk + b, v_col + kh)),
        ],
        out_specs=pl.BlockSpec((tq, group_w), lambda b, kh, t: (q_rows(b, kh, t), kh)),
        out_shape=jax.ShapeDtypeStruct((R_ALL, GQA_HEADS * HEAD_DIM), BF16),
        scratch_shapes=[pltpu.VMEM((SEQ, 2 * HEAD_DIM), BF16),
                        pltpu.VMEM((CTX_LEN, 2 * HEAD_DIM), BF16)],
        compiler_params=_cparams(("parallel", "parallel", "arbitrary")),
        name="gqa_attention",
    )(z, z, z, z, z)


def _split3(x):
    hi = x.astype(BF16)
    r1 = x - hi.astype(F32)
    mid = r1.astype(BF16)
    lo = (r1 - mid.astype(F32)).astype(BF16)
    return hi, mid, lo


def _log_sigmoid(x):
    return jnp.minimum(x, 0.0) - jnp.log(1.0 + jnp.exp(-jnp.abs(x)))


MLSTM_AUG = MLSTM_DV + 128


def _mlstm_kernel(q_ref, k_ref, v_ref, g_ref, o_ref, c_scr, m_scr):
    d = pl.program_id(1)
    c = pl.program_id(2)
    nh = MLSTM_HEADS
    L = SEG
    scale = MLSTM_DQK ** -0.5

    @pl.when(c == 0)
    def _():
        c_scr[...] = jnp.zeros_like(c_scr)
        m_scr[...] = jnp.full_like(m_scr, M_INIT)

    ri = lax.broadcasted_iota(jnp.int32, (L, L), 0)
    ci = lax.broadcasted_iota(jnp.int32, (L, L), 1)
    diff = jnp.where(d == 0, ci - ri, ri - ci)
    sees = diff <= 0
    cum_t = jnp.where(diff >= 0, 1.0, 0.0).astype(BF16)

    g = g_ref[...]
    ig = g[0:nh]
    lf = _log_sigmoid(g[nh:2 * nh])
    b = sum(_dot(p, cum_t) for p in _split3(lf))
    g_tot = jnp.sum(lf, axis=1, keepdims=True)
    m_in = m_scr[:, 0:1]
    cr = ig - b
    lane = lax.broadcasted_iota(jnp.int32, (nh, L), 1)
    pm = cr
    for step in range(L.bit_length() - 1):
        sh = 1 << step
        prev = jnp.where(lane >= sh, pltpu.roll(pm, sh, 1), NEG)
        nxt = jnp.where(lane < L - sh, pltpu.roll(pm, L - sh, 1), NEG)
        pm = jnp.maximum(pm, jnp.where(d == 0, prev, nxt))
    big_m = jnp.maximum(m_in, pm)
    w_inter = jnp.exp(m_in - big_m) * scale
    floor = jnp.exp(-(b + big_m))
    a = g_tot - b + ig
    m_loc = jnp.max(a, axis=1, keepdims=True)
    w = jnp.exp(a - m_loc)
    m_new = jnp.maximum(g_tot + m_in, m_loc)
    a_old = jnp.exp(g_tot + m_in - m_new)
    a_new = jnp.exp(m_loc - m_new)
    slab = jnp.concatenate([big_m, w_inter, floor, jnp.zeros((128 - 3 * nh, L), F32)], axis=0)
    cols = slab.T

    ones = jnp.ones((L, MLSTM_AUG - MLSTM_DV), BF16)
    for h in range(nh):
        q = q_ref[:, h * MLSTM_DQK:(h + 1) * MLSTM_DQK]
        k = k_ref[:, h * MLSTM_DQK:(h + 1) * MLSTM_DQK]
        v_aug = jnp.concatenate([v_ref[:, h * MLSTM_DV:(h + 1) * MLSTM_DV], ones], axis=1)
        e = jnp.where(sees, cr[h:h + 1, :] - cols[:, h:h + 1], NEG)
        s = _dot_nt(q, k) * (jnp.exp(e) * scale)
        c_in = c_scr[h]
        num = _dot(s.astype(BF16), v_aug) + cols[:, nh + h:nh + h + 1] * _dot(q, c_in.astype(BF16))
        den = jnp.maximum(jnp.abs(num[:, MLSTM_DV:MLSTM_DV + 1]), cols[:, 2 * nh + h:2 * nh + h + 1])
        o_ref[:, h * MLSTM_DV:(h + 1) * MLSTM_DV] = num[:, :MLSTM_DV] / den

        kw_t = (k.astype(F32).T * w[h:h + 1, :]).astype(BF16)
        c_scr[h] = a_old[h:h + 1, :] * c_in + a_new[h:h + 1, :] * _dot(kw_t, v_aug)
    m_scr[...] = jnp.broadcast_to(m_new, m_scr.shape)


def _mlstm(z, gates_t):
    n_lat = SEQ // SEG
    ctx_blk = R_LAT // SEG
    nqk = MLSTM_HEADS * MLSTM_DQK
    nv = MLSTM_HEADS * MLSTM_DV

    def rb(b, d, c):
        lat = jnp.where(d == 0, c - 1, n_lat - c)
        return jnp.where(c == 0, ctx_blk + b, b * n_lat + lat)

    return pl.pallas_call(
        _mlstm_kernel,
        grid=(BATCH, 2, n_lat + 1),
        in_specs=[
            pl.BlockSpec((SEG, nqk), lambda b, d, c: (rb(b, d, c), 0)),
            pl.BlockSpec((SEG, nqk), lambda b, d, c: (rb(b, d, c), 1)),
            pl.BlockSpec((SEG, nv), lambda b, d, c: (rb(b, d, c), 2 * nqk // nv)),
            pl.BlockSpec((2 * MLSTM_HEADS, SEG), lambda b, d, c: (d, rb(b, d, c))),
        ],
        out_specs=pl.BlockSpec((None, SEG, nv), lambda b, d, c: (d, rb(b, d, c), 0)),
        out_shape=jax.ShapeDtypeStruct((2, R_ALL, nv), F32),
        scratch_shapes=[pltpu.VMEM((MLSTM_HEADS, MLSTM_DQK, MLSTM_AUG), F32),
                        pltpu.VMEM((MLSTM_HEADS, 128), F32)],
        compiler_params=_cparams(("parallel", "parallel", "arbitrary")),
        name="mlstm",
    )(z, z, z, gates_t)


DW_ROWS = 32
DW_COLS = 512


def _dwconv_ln_kernel(u_ref, up_ref, un_ref, w_ref, b_ref, lw_ref, lb_ref, o_ref, pad_scr, conv_scr, sh_scr):
    i = pl.program_id(0)
    tm = u_ref.shape[0]
    r0 = i * tm
    keep_prev = jnp.logical_not(_is_seq_start(r0))
    keep_next = jnp.logical_not(_is_seq_start(r0 + tm))
    pad_scr[0:HALO, :] = jnp.where(keep_prev, up_ref[...].astype(F32), 0.0)
    pad_scr[HALO:HALO + tm, :] = u_ref[...].astype(F32)
    pad_scr[HALO + tm:HALO + tm + HALO, :] = jnp.where(keep_next, un_ref[...].astype(F32), 0.0)

    rc = DW_ROWS
    cw = DW_COLS
    half = CONV_WIDTH // 2
    n_sh = sh_scr.shape[1]

    def chunk(cs, carry):
        cl = pl.ds(pl.multiple_of(cs * cw, cw), cw)
        for s in range(1, 8):
            sh_scr[s] = pad_scr[s:s + n_sh, cl]
        for ib in range(tm // rc):
            base = ib * rc
            acc = jnp.zeros((rc, cw), F32) + b_ref[:, cl]
            for kk in range(CONV_WIDTH):
                al, s = divmod(HALO - half + kk, 8)
                off = base + 8 * al
                tap = pad_scr[off:off + rc, cl] if s == 0 else sh_scr[s, off:off + rc, :]
                acc = acc + tap * w_ref[kk:kk + 1, cl]
            conv_scr[base:base + rc, cl] = acc
        return carry

    lax.fori_loop(0, D_MODEL // cw, chunk, 0)
    y = conv_scr[...]
    mu = jnp.mean(y, axis=-1, keepdims=True)
    yc = y - mu
    var = jnp.mean(yc * yc, axis=-1, keepdims=True)
    yn = yc * lax.rsqrt(var + EPS) * lw_ref[...] + lb_ref[...]
    o_ref[...] = (yn * _sigmoid(yn)).astype(o_ref.dtype)


def _dwconv_ln_swish(u, w_dw, b_dw, ln_w, ln_b):
    rows = u.shape[0]
    tm = SEG
    hb = tm // HALO
    last_halo = rows // HALO - 1
    vec_spec = pl.BlockSpec((1, D_MODEL), lambda i: (0, 0))
    return pl.pallas_call(
        _dwconv_ln_kernel,
        grid=(rows // tm,),
        in_specs=[
            pl.BlockSpec((tm, D_MODEL), lambda i: (i, 0)),
            pl.BlockSpec((HALO, D_MODEL), lambda i: (jnp.maximum(i * hb - 1, 0), 0)),
            pl.BlockSpec((HALO, D_MODEL), lambda i: (jnp.minimum((i + 1) * hb, last_halo), 0)),
            pl.BlockSpec((CONV_WIDTH, D_MODEL), lambda i: (0, 0)),
            vec_spec, vec_spec, vec_spec,
        ],
        out_specs=pl.BlockSpec((tm, D_MODEL), lambda i: (i, 0)),
        out_shape=jax.ShapeDtypeStruct((rows, D_MODEL), BF16),
        scratch_shapes=[pltpu.VMEM((tm + 2 * HALO, D_MODEL), F32),
                        pltpu.VMEM((tm, D_MODEL), F32),
                        pltpu.VMEM((8, tm + 2 * HALO - 8, DW_COLS), F32)],
        compiler_params=_cparams(("parallel",)),
        name="dwconv_ln_swish",
    )(u, u, u, w_dw, b_dw, ln_w, ln_b)


NAT_Q_ROWS = 4
NAT_BAND_ROWS = 12
NAT_TQ = NAT_Q_ROWS * GRID_W
NAT_BAND = NAT_BAND_ROWS * GRID_W


NAT_HEADS_PER_STEP = 4


def _nat_kernel(q_ref, kl_ref, vl_ref, kc_ref, vc_ref, bias_ref, o_ref, vla_scr, vca_scr):
    gq = pl.program_id(2)
    max_start = (SEQ - NAT_BAND) // NAT_TQ
    start = pl.multiple_of(jnp.clip(gq - 1, 0, max_start) * NAT_TQ, NAT_TQ)
    hd = NA_HEAD_DIM

    @pl.when(gq == 0)
    def _():
        for hh in range(NAT_HEADS_PER_STEP):
            sl = slice(hh * hd, (hh + 1) * hd)
            vla_scr[:, 2 * hh * hd:2 * (hh + 1) * hd] = _with_ones(vl_ref[:, sl])
            vca_scr[:, 2 * hh * hd:2 * (hh + 1) * hd] = _with_ones(vc_ref[:, sl])

    def scores(hh):
        sl = slice(hh * hd, (hh + 1) * hd)
        q = q_ref[:, sl]
        return _dot_nt(q, kl_ref[pl.ds(start, NAT_BAND), sl]) + bias_ref[hh], _dot_nt(q, kc_ref[:, sl])

    s_next = scores(0)
    for hh in range(NAT_HEADS_PER_STEP):
        sl = slice(hh * hd, (hh + 1) * hd)
        sl2 = slice(2 * hh * hd, 2 * (hh + 1) * hd)
        s_n, s_c = s_next
        if hh + 1 < NAT_HEADS_PER_STEP:
            s_next = scores(hh + 1)
        m = jnp.maximum(jnp.max(s_n, axis=1, keepdims=True), jnp.max(s_c, axis=1, keepdims=True))
        p_n = _exp_bf16(s_n - m)
        p_c = _exp_bf16(s_c - m)
        o = _dot(p_n, vla_scr[pl.ds(start, NAT_BAND), sl2]) + _dot(p_c, vca_scr[:, sl2])
        o_ref[:, sl] = (o[:, :hd] / o[:, hd:hd + 1]).astype(o_ref.dtype)


def _nat_bias_blocks():
    rows = SEQ // GRID_W
    n_groups = rows // NAT_Q_ROWS
    q_col = np.arange(GRID_W)
    c_start = np.clip(q_col - NA_COLS // 2, 0, GRID_W - NA_COLS)
    dc = q_col[None, :] - c_start[:, None]
    col_ok = (dc >= 0) & (dc < NA_COLS)
    masked = 2 * NA_ROWS - 1
    block = np.full((3, NAT_Q_ROWS, NAT_BAND_ROWS), masked, np.int64)
    for var, gq in enumerate((0, 1, n_groups - 1)):
        lo = int(np.clip(gq - 1, 0, (SEQ - NAT_BAND) // NAT_TQ)) * NAT_Q_ROWS
        for qr in range(NAT_Q_ROWS):
            q_row = gq * NAT_Q_ROWS + qr
            r_start = int(np.clip(q_row - NA_ROWS // 2, 0, rows - NA_ROWS))
            for kr in range(NAT_BAND_ROWS):
                k_row = lo + kr
                if 0 <= k_row - r_start < NA_ROWS:
                    block[var, qr, kr] = k_row - q_row + (NA_ROWS - 1)
    return col_ok, block


def _nat_bias(rpb):
    n_r, n_c = 2 * NA_ROWS - 1, 2 * NA_COLS - 1
    w = GRID_W
    col_ok, block = _nat_bias_blocks()
    lead = w - NA_COLS
    vext = jnp.pad(rpb, ((0, 0), (0, 0), (lead, 2 * w - lead - n_c)))
    skew = jnp.broadcast_to(vext[:, :, None, :], (NA_HEADS, n_r, w, 2 * w)).reshape(NA_HEADS, n_r, 2 * w * w)
    skew = skew[:, :, :w * (2 * w - 1)].reshape(NA_HEADS, n_r, w, 2 * w - 1)
    tb = jnp.where(col_ok[None, None], skew[..., w - 1:], NEG)
    tb = jnp.concatenate([tb, jnp.full((NA_HEADS, 1, w, w), NEG, F32)], axis=1)
    strips = [jnp.concatenate([tb[:, int(d)] for d in block[var, qr]], axis=-1)
              for var in range(3) for qr in range(NAT_Q_ROWS)]
    return jnp.stack(strips, axis=1).reshape(NA_HEADS, 3, NAT_TQ, NAT_BAND)


def _nat_attention(z, rpb):
    n_groups = SEQ // NAT_TQ
    ctx_blk = R_LAT // SEG
    bias = _nat_bias(rpb)
    hps = NAT_HEADS_PER_STEP
    gw = hps * NA_HEAD_DIM

    def variant(gq):
        return (gq > 0).astype(jnp.int32) + (gq == n_groups - 1).astype(jnp.int32)

    k_col = NA_HEADS // hps
    v_col = 2 * NA_HEADS // hps
    return pl.pallas_call(
        _nat_kernel,
        grid=(BATCH, NA_HEADS // hps, n_groups),
        in_specs=[
            pl.BlockSpec((NAT_TQ, gw), lambda b, h, gq: (b * n_groups + gq, h)),
            pl.BlockSpec((SEQ, gw), lambda b, h, gq: (b, k_col + h)),
            pl.BlockSpec((SEQ, gw), lambda b, h, gq: (b, v_col + h)),
            pl.BlockSpec((CTX_LEN, gw), lambda b, h, gq: (ctx_blk + b, k_col + h)),
            pl.BlockSpec((CTX_LEN, gw), lambda b, h, gq: (ctx_blk + b, v_col + h)),
            pl.BlockSpec((hps, None, NAT_TQ, NAT_BAND), lambda b, h, gq: (h, variant(gq), 0, 0)),
        ],
        out_specs=pl.BlockSpec((NAT_TQ, gw), lambda b, h, gq: (b * n_groups + gq, h)),
        out_shape=jax.ShapeDtypeStruct((R_LAT, NA_HEADS * NA_HEAD_DIM), BF16),
        scratch_shapes=[pltpu.VMEM((SEQ, 2 * gw), BF16),
                        pltpu.VMEM((CTX_LEN, 2 * gw), BF16)],
        compiler_params=_cparams(("parallel", "parallel", "arbitrary")),
        name="nat_attention",
    )(z, z, z, z, z, bias)


def _rope_tables():
    pos = jnp.arange(SEQ)
    quarter = HEAD_DIM // 4
    freqs = ROPE_THETA ** (-jnp.arange(quarter, dtype=F32) / quarter)

    def cs(p):
        ang = p.astype(F32)[:, None] * freqs[None, :]
        return jnp.cos(ang), jnp.sin(ang)

    cr, sr = cs(pos // GRID_W)
    cc, sc = cs(pos % GRID_W)
    cos = jnp.concatenate([cr, cr, cc, cc], axis=1)
    sins = jnp.concatenate([-sr, sr, -sc, sc], axis=1)
    n_ctx = BATCH * CTX_LEN
    cos = jnp.concatenate([cos] * BATCH + [jnp.ones((n_ctx, HEAD_DIM), F32)], axis=0)
    sins = jnp.concatenate([sins] * BATCH + [jnp.zeros((n_ctx, HEAD_DIM), F32)], axis=0)
    return cos, sins


def kernel(x, c, ctx, c_ctx, mod_w, mod_b, norm_pre_mix, norm_post_mix, norm_pre_ffn, norm_post_ffn, ffn_w_up, ffn_conv_w, ffn_conv_b, ffn_w_down, gqa_w_in, gqa_q_norm, gqa_k_norm, gqa_w_out, mlstm_w_in, mlstm_b_gate, mlstm_out_norm, mlstm_w_out, conv_w_pw1, conv_b_pw1, conv_w_dw, conv_b_dw, conv_ln_w, conv_ln_b, conv_w_pw2, conv_b_pw2, nat_w_in, nat_rpb, nat_w_out):
    tm = 512
    xs = jnp.concatenate([x.reshape(R_LAT, D_MODEL), ctx.reshape(BATCH * CTX_LEN, D_MODEL)], axis=0)
    a8 = jnp.concatenate([c, c_ctx[None, :], jnp.zeros((8 - BATCH - 1, D_MODEL), F32)], axis=0)
    mods = _modulation(a8, mod_w, mod_b)
    mods = mods[:, :BATCH + 1].reshape(DEPTH, BATCH + 1, MOD_CHUNKS, 1, D_MODEL)
    cos, sins = _rope_tables()
    w_up_bf16 = ffn_w_up.astype(BF16)
    w_down_bf16 = ffn_w_down.astype(BF16)
    conv_p = _ffn_conv_params(ffn_conv_w, ffn_conv_b)

    def row(v):
        return v.reshape(1, -1)

    for i in range(DEPTH):
        kind = i % 4
        last = i == DEPTH - 1
        rows_out = R_LAT if last else R_ALL
        sh1, sc1, g1, sh2, sc2, g2 = [mods[i, :, m] for m in range(MOD_CHUNKS)]
        pre = row(norm_pre_mix[i])
        post = row(norm_post_mix[i])
        x_dead = xs
        if kind == 0:
            z = _project(xs, pre, sh1, sc1, gqa_w_in[0].astype(BF16), mode="gqa", tm=tm, tn=1024,
                         extra=(row(gqa_q_norm[0]), row(gqa_k_norm[0]), cos, sins), name="gqa_in")
            a = _gqa_attention(z)
            xs = _out_project(a, gqa_w_out[0].astype(BF16), None, xs, post, g1, tm=tm, rows_out=rows_out,
                              name="gqa_out")
        elif kind == 1:
            n_main = 2 * MLSTM_HEADS * MLSTM_DQK + 2 * MLSTM_HEADS * MLSTM_DV
            n_gate = 4 * MLSTM_HEADS
            w_in = mlstm_w_in[0]
            w_g = jnp.pad(w_in[:, n_main:], ((0, 0), (0, 128 - n_gate))).astype(BF16)
            b_g = jnp.pad(mlstm_b_gate[0], (0, 128 - n_gate)).reshape(1, 128)
            z, gates_t = _project(xs, pre, sh1, sc1, w_in[:, :n_main].astype(BF16), mode="mlstm", tm=tm, tn=2048,
                                  extra=(w_g, b_g), name="mlstm_in")
            hdir = _mlstm(z, gates_t)
            xs = _out_project(None, mlstm_w_out[0].astype(BF16), None, xs, post, g1, tm=SEG, rows_out=rows_out,
                              name="mlstm_out", mlstm=(hdir, z, row(mlstm_out_norm[0])))
        elif kind == 2:
            u = _project(xs, pre, sh1, sc1, conv_w_pw1[0].astype(BF16), mode="glu", tm=tm, tn=1024,
                         bias=row(conv_b_pw1[0]), name="conformer_pw1")
            a = _dwconv_ln_swish(u, conv_w_dw[0], row(conv_b_dw[0]), row(conv_ln_w[0]), row(conv_ln_b[0]))
            xs = _out_project(a, conv_w_pw2[0].astype(BF16), row(conv_b_pw2[0]), xs, post, g1, tm=tm,
                              rows_out=rows_out, name="conformer_pw2")
        else:
            nq = NA_HEADS * NA_HEAD_DIM
            z = _project(xs, pre, sh1, sc1, nat_w_in[0].astype(BF16), mode="plain", tm=tm, tn=2048,
                         n_scaled=nq // 2048, scale=NA_HEAD_DIM ** -0.5, name="nat_in")
            a = _nat_attention(z, nat_rpb[0])
            xs = _out_project(a, nat_w_out[0].astype(BF16), None, xs, post, g1, tm=tm, rows_out=rows_out,
                              name="nat_out")
        ffn_args = (xs, row(norm_pre_ffn[i]), sh2, sc2, g2, w_up_bf16, conv_p, w_down_bf16, row(norm_post_ffn[i]))
        x_new = _conv_ffn(*ffn_args, layer=i, tm=FFN_TM_LATENT, tf=FFN_TF, rows_out=rows_out,
                          n_tiles=R_LAT // FFN_TM_LATENT, into=None if last else x_dead, name="conv_ffn")
        if not last:
            x_new = _conv_ffn(*ffn_args, layer=i, tm=tm, tf=FFN_TF, rows_out=rows_out, tile0=R_LAT // tm,
                              n_tiles=(R_ALL - R_LAT) // tm, into=x_new, name="conv_ffn_ctx")
        xs = x_new
    return xs.reshape(BATCH, SEQ, D_MODEL)
```
